```python
import jax, jax.numpy as jnp
from jax import lax
import numpy as np

D_MODEL = 2048
BATCH = 16
SEQ = 256
DEPTH = 4
DEC_BATCH = 4
DEC_SEQ = 1024
PAST_LEN = 512

GRID_W = 64
DN_HEADS = 8
DN_DK = 128
DN_DV = 128
CONV_W = 5
CHUNK = 64
MLA_HEADS = 8
Q_LORA = 512
KV_LORA = 256
NOPE_DIM = 128
ROPE_DIM = 64
V_DIM = 128
QK_DIM = NOPE_DIM + ROPE_DIM
ROPE_BASE = 10000.0
Q_BLOCK = 128
DN_QKV_COLS = 2 * DN_HEADS * DN_DK + DN_HEADS * DN_DV
IN_WIDTHS = (DN_QKV_COLS, DN_HEADS * DN_DV, 2 * DN_HEADS, 2 * DN_HEADS, Q_LORA, KV_LORA, ROPE_DIM)
IN_COLS = sum(IN_WIDTHS)
MIX_WIDTH = DN_HEADS * DN_DV + MLA_HEADS * V_DIM
N_EXPERTS = 16
N_GROUPS = 4
EXPERTS_PER_GROUP = N_EXPERTS // N_GROUPS
TOP_K = 2
D_FF_EXPERT = 512
N_MOD = 6
EPS = 1e-6

kernel_name = 'hymba_deltanet_mla_moe_diffusion_step'

F32 = jnp.float32


def rms_norm(x, g):
    xf = x.astype(F32)
    y = xf * lax.rsqrt(jnp.mean(xf * xf, axis=-1, keepdims=True) + EPS)
    return (y * g.astype(F32)).astype(x.dtype)


def l2_normalize(x):
    return x * lax.rsqrt(jnp.sum(x * x, axis=-1, keepdims=True) + EPS)


def modulation(cond, w_mod, b_mod):
    m = jax.nn.silu(cond) @ w_mod + b_mod
    return jnp.split(m[..., None, :], N_MOD, axis=-1)


def split_in_proj(proj):
    cuts = np.cumsum(IN_WIDTHS)[:-1].tolist()
    return jnp.split(proj, cuts, axis=-1)


def centred_depthwise_conv(x, w):
    return lax.conv_general_dilated(x, w[:, None, :], window_strides=(1,), padding=[(CONV_W // 2, CONV_W // 2)],
                                    dimension_numbers=('NWC', 'WIO', 'NWC'), feature_group_count=x.shape[-1])


def chunk_gated_delta(q, k, v, g, beta, s0):
    bsz, t, h, dk = q.shape
    n = t // CHUNK

    def to_chunks(a):
        return jnp.moveaxis(a.reshape(bsz, n, CHUNK, h, -1), 3, 1)

    qc = to_chunks(q) * (dk ** -0.5)
    kc = to_chunks(k)
    vc = to_chunks(v)
    gc = jnp.cumsum(to_chunks(g[..., None])[..., 0], axis=-1)
    bc = to_chunks(beta[..., None])
    i = jnp.arange(CHUNK)
    incl = i[:, None] >= i[None, :]
    strict = i[:, None] > i[None, :]
    decay = jnp.exp(jnp.where(incl, gc[..., :, None] - gc[..., None, :], -jnp.inf))
    kb = kc * bc
    lmat = jnp.where(strict, jnp.einsum('bhncd,bhnsd->bhncs', kb, kc) * decay, 0.0)
    eye = jnp.eye(CHUNK, dtype=lmat.dtype)
    tmat = lax.linalg.triangular_solve(eye + lmat, jnp.broadcast_to(eye, lmat.shape), left_side=True, lower=True)
    u = tmat @ (vc * bc)
    w = tmat @ (kb * jnp.exp(gc)[..., None])
    attn = jnp.where(incl, jnp.einsum('bhncd,bhnsd->bhncs', qc, kc) * decay, 0.0)
    q_dec = qc * jnp.exp(gc)[..., None]
    k_dec = kc * jnp.exp(gc[..., -1:] - gc)[..., None]
    g_last = jnp.exp(gc[..., -1])
    xs = tuple(jnp.moveaxis(a, 2, 0) for a in (u, w, attn, q_dec, k_dec, g_last))

    def step(s, inp):
        u_i, w_i, a_i, qd_i, kd_i, gl_i = inp
        v_new = u_i - w_i @ s
        o_i = qd_i @ s + a_i @ v_new
        s = s * gl_i[..., None, None] + jnp.swapaxes(kd_i, -1, -2) @ v_new
        return s, o_i

    s_fin, o = lax.scan(step, s0, xs)
    o = jnp.transpose(o, (1, 0, 3, 2, 4)).reshape(bsz, t, h, -1)
    return o, s_fin


def gated_deltanet(qkv, z, a, b, conv_w, a_log, dt_bias, g_out, s0):
    bsz, t, _ = qkv.shape
    qkv = jax.nn.silu(centred_depthwise_conv(qkv, conv_w)).astype(F32)
    q, k, v = jnp.split(qkv, [DN_HEADS * DN_DK, 2 * DN_HEADS * DN_DK], axis=-1)
    q = l2_normalize(q.reshape(bsz, t, DN_HEADS, DN_DK))
    k = l2_normalize(k.reshape(bsz, t, DN_HEADS, DN_DK))
    v = v.reshape(bsz, t, DN_HEADS, DN_DV)
    a = a.astype(F32).reshape(bsz, t, 2, DN_HEADS)
    b = b.astype(F32).reshape(bsz, t, 2, DN_HEADS)
    g = -jnp.exp(a_log.astype(F32)) * jax.nn.softplus(a + dt_bias.astype(F32))
    beta = jax.nn.sigmoid(b)
    s0 = s0.astype(F32)
    o_f, s_f = chunk_gated_delta(q, k, v, g[:, :, 0], beta[:, :, 0], s0[:, 0])
    o_b, s_b = chunk_gated_delta(q[:, ::-1], k[:, ::-1], v[:, ::-1], g[:, ::-1, 1], beta[:, ::-1, 1], s0[:, 1])
    o = o_f + o_b[:, ::-1]
    o = rms_norm(o, g_out) * jax.nn.silu(z.astype(F32).reshape(bsz, t, DN_HEADS, DN_DV))
    return o.reshape(bsz, t, DN_HEADS * DN_DV).astype(z.dtype), jnp.stack([s_f, s_b], axis=1)


def axial_rope(rows):
    t = jnp.arange(rows * GRID_W)
    row = (t // GRID_W).astype(F32)
    col = (t % GRID_W).astype(F32)
    quarter = ROPE_DIM // 4
    inv = ROPE_BASE ** (-jnp.arange(quarter, dtype=F32) / quarter)
    ang_r = row[:, None] * inv
    ang_c = col[:, None] * inv
    return (jnp.cos(ang_r)[:, None, :], jnp.sin(ang_r)[:, None, :], jnp.cos(ang_c)[:, None, :], jnp.sin(ang_c)[:, None, :])


def rotate(x, cos, sin):
    x1, x2 = jnp.split(x, 2, axis=-1)
    return jnp.concatenate([x1 * cos - x2 * sin, x1 * sin + x2 * cos], axis=-1)


def apply_axial_rope(x, rope):
    cos_r, sin_r, cos_c, sin_c = rope
    x_nope, x_row, x_col = jnp.split(x, [NOPE_DIM, NOPE_DIM + ROPE_DIM // 2], axis=-1)
    out = jnp.concatenate([x_nope.astype(F32), rotate(x_row.astype(F32), cos_r, sin_r), rotate(x_col.astype(F32), cos_c, sin_c)], axis=-1)
    return out.astype(x.dtype)


def mla_queries(cq, g_qa, w_uq, g_qh):
    bsz, t, _ = cq.shape
    q = (rms_norm(cq, g_qa) @ w_uq).reshape(bsz, t, MLA_HEADS, QK_DIM)
    return rms_norm(q, g_qh)


def mla_keys_values(ckv, krope, w_ukv, g_kh):
    bsz, s, _ = ckv.shape
    kv = (ckv @ w_ukv).reshape(bsz, s, MLA_HEADS, NOPE_DIM + V_DIM)
    k_nope, v = jnp.split(kv, [NOPE_DIM], axis=-1)
    k = jnp.concatenate([k_nope, jnp.broadcast_to(krope[:, :, None, :], (bsz, s, MLA_HEADS, ROPE_DIM)).astype(k_nope.dtype)], axis=-1)
    return rms_norm(k, g_kh), v


def block_attention(q, k, v):
    bsz, t, h, d = q.shape
    qb = jnp.moveaxis(q.reshape(bsz, t // Q_BLOCK, Q_BLOCK, h, d), 1, 0)

    def one_block(qi):
        s = jnp.einsum('bqhd,bkhd->bhqk', qi, k).astype(F32) * (d ** -0.5)
        p = jax.nn.softmax(s, axis=-1).astype(v.dtype)
        return jnp.einsum('bhqk,bkhd->bqhd', p, v)

    o = lax.map(one_block, qb)
    return jnp.moveaxis(o, 0, 1).reshape(bsz, t, h * v.shape[-1])


def grouped_moe(h, w_router, b_router, w_gate, w_up, w_down):
    bsz, t, d = h.shape
    x = h.reshape(-1, d)
    scores = jax.nn.sigmoid((x @ w_router).astype(F32))
    sel = (scores + b_router.astype(F32)).reshape(-1, N_GROUPS, EXPERTS_PER_GROUP)
    group_score = jnp.sum(lax.top_k(sel, 2)[0], axis=-1)
    group = jnp.argmax(group_score, axis=-1)
    in_group = jnp.take_along_axis(sel, group[:, None, None], axis=1)[:, 0]
    _, local = lax.top_k(in_group, TOP_K)
    expert = group[:, None] * EXPERTS_PER_GROUP + local
    gate = jnp.take_along_axis(scores, expert, axis=-1)
    gate = gate / jnp.sum(gate, axis=-1, keepdims=True)
    combine = jnp.einsum('nk,nke->ne', gate, jax.nn.one_hot(expert, N_EXPERTS, dtype=F32)).astype(h.dtype)
    hg = jnp.einsum('nd,edf->nef', x, w_gate)
    hu = jnp.einsum('nd,edf->nef', x, w_up)
    act = jax.nn.silu(hg) * hu * combine[..., None]
    out = jnp.einsum('nef,efd->nd', act, w_down)
    return out.reshape(bsz, t, d).astype(h.dtype)


def trunk_layer(x, mods, lw, rope, ctx_kv, s0, w_router, b_router):
    shift1, scale1, gate1, shift2, scale2, gate2 = mods
    bsz, t, _ = x.shape
    h = rms_norm(x, lw['g_mix']) * (1 + scale1) + shift1
    qkv, z, a, b, cq, ckv, krope = split_in_proj(h @ lw['w_in'])
    dn_out, s_fin = gated_deltanet(qkv, z, a, b, lw['conv_w'], lw['a_log'], lw['dt_bias'], lw['g_dn_out'], s0)
    ckv = rms_norm(ckv, lw['g_kva'])
    q = mla_queries(cq, lw['g_qa'], lw['w_uq'], lw['g_qh'])
    k, v = mla_keys_values(ckv, krope, lw['w_ukv'], lw['g_kh'])
    if rope is not None:
        q = apply_axial_rope(q, rope)
        k = apply_axial_rope(k, rope)
    if ctx_kv is not None:
        k_ctx, v_ctx = mla_keys_values(ctx_kv[0], ctx_kv[1], lw['w_ukv'], lw['g_kh'])
        k = jnp.concatenate([k_ctx.astype(k.dtype), k], axis=1)
        v = jnp.concatenate([v_ctx.astype(v.dtype), v], axis=1)
    mla_out = block_attention(q, k, v)
    x = x + gate1 * (jnp.concatenate([dn_out, mla_out], axis=-1) @ lw['w_out'])
    h = rms_norm(x, lw['g_ffn']) * (1 + scale2) + shift2
    x = x + gate2 * grouped_moe(h, w_router, b_router, lw['w_gate'], lw['w_up'], lw['w_down'])
    return x, ckv, krope, s_fin


def setup_inputs(seed: int = 0) -> dict:
    key = jax.random.key(seed)
    ks = jax.random.split(key, 32)
    nrm = jax.random.normal
    d = D_MODEL

    def gain(k, shape):
        return 1.0 + 0.02 * nrm(k, shape, F32)

    dt = jnp.exp(jax.random.uniform(ks[12], (DEPTH, 2, DN_HEADS), F32, np.log(1e-3), np.log(1e-1)))
    return {
        'x_prompt': nrm(ks[0], (BATCH, SEQ, d), F32),
        'x_sample': nrm(ks[1], (DEC_BATCH, DEC_SEQ, d), F32),
        'cache_ckv': nrm(ks[2], (DEC_BATCH, DEPTH, PAST_LEN, KV_LORA), F32),
        'cache_krope': nrm(ks[3], (DEC_BATCH, DEPTH, PAST_LEN, ROPE_DIM), F32),
        'state_delta': 0.1 * nrm(ks[4], (DEC_BATCH, DEPTH, 2, DN_HEADS, DN_DK, DN_DV), F32),
        'c': nrm(ks[5], (DEC_BATCH, d), F32),
        'c_ctx': nrm(ks[6], (d,), F32),
        'g_mix': gain(ks[7], (DEPTH, d)),
        'w_mod': 0.5 * d ** -0.5 * nrm(ks[8], (DEPTH, d, N_MOD * d), F32),
        'b_mod': 0.02 * nrm(ks[9], (DEPTH, N_MOD * d), F32),
        'w_in': d ** -0.5 * nrm(ks[10], (DEPTH, d, IN_COLS), F32),
        'conv_w': CONV_W ** -0.5 * nrm(ks[11], (DEPTH, CONV_W, DN_QKV_COLS), F32),
        'a_log': jnp.log(jax.random.uniform(ks[13], (DEPTH, 2, DN_HEADS), F32, 1.0, 16.0)),
        'dt_bias': dt + jnp.log(-jnp.expm1(-dt)),
        'g_dn_out': gain(ks[14], (DEPTH, DN_DV)),
        'g_qa': gain(ks[15], (DEPTH, Q_LORA)),
        'w_uq': Q_LORA ** -0.5 * nrm(ks[16], (DEPTH, Q_LORA, MLA_HEADS * QK_DIM), F32),
        'g_kva': gain(ks[17], (DEPTH, KV_LORA)),
        'w_ukv': KV_LORA ** -0.5 * nrm(ks[18], (DEPTH, KV_LORA, MLA_HEADS * (NOPE_DIM + V_DIM)), F32),
        'g_qh': gain(ks[19], (DEPTH, QK_DIM)),
        'g_kh': gain(ks[20], (DEPTH, QK_DIM)),
        'w_out': MIX_WIDTH ** -0.5 * nrm(ks[21], (DEPTH, MIX_WIDTH, d), F32),
        'g_ffn': gain(ks[22], (DEPTH, d)),
        'w_router': d ** -0.5 * nrm(ks[23], (d, N_EXPERTS), F32),
        'b_router': 0.01 * nrm(ks[24], (N_EXPERTS,), F32),
        'w_gate': d ** -0.5 * nrm(ks[25], (DEPTH, N_EXPERTS, d, D_FF_EXPERT), F32),
        'w_up': d ** -0.5 * nrm(ks[26], (DEPTH, N_EXPERTS, d, D_FF_EXPERT), F32),
        'w_down': D_FF_EXPERT ** -0.5 * nrm(ks[27], (DEPTH, N_EXPERTS, D_FF_EXPERT, d), F32),
    }


def reference(x_prompt, x_sample, cache_ckv, cache_krope, state_delta, c, c_ctx, g_mix, w_mod, b_mod, w_in, conv_w,
              a_log, dt_bias, g_dn_out, g_qa, w_uq, g_kva, w_ukv, g_qh, g_kh, w_out, g_ffn, w_router, b_router,
              w_gate, w_up, w_down):
    rows = x_sample.shape[1] // GRID_W
    rope = axial_rope(rows)
    s0_ctx = jnp.zeros((x_prompt.shape[0], 2, DN_HEADS, DN_DK, DN_DV), F32)
    yp, ys = x_prompt, x_sample
    ckv_list, krope_list, state_list = [], [], []
    for l in range(DEPTH):
        lw = dict(g_mix=g_mix[l], w_in=w_in[l], conv_w=conv_w[l], a_log=a_log[l], dt_bias=dt_bias[l],
                  g_dn_out=g_dn_out[l], g_qa=g_qa[l], w_uq=w_uq[l], g_kva=g_kva[l], w_ukv=w_ukv[l],
                  g_qh=g_qh[l], g_kh=g_kh[l], w_out=w_out[l], g_ffn=g_ffn[l], w_gate=w_gate[l],
                  w_up=w_up[l], w_down=w_down[l])
        yp, ckv_l, krope_l, s_l = trunk_layer(yp, modulation(c_ctx, w_mod[l], b_mod[l]), lw, None, None, s0_ctx,
                                              w_router, b_router)
        ckv_list.append(ckv_l)
        krope_list.append(krope_l)
        state_list.append(s_l)
        ys, _, _, _ = trunk_layer(ys, modulation(c, w_mod[l], b_mod[l]), lw, rope,
                                  (cache_ckv[:, l], cache_krope[:, l]), state_delta[:, l], w_router, b_router)
    new_cache_ckv = jnp.stack(ckv_list, axis=1)
    new_cache_krope = jnp.stack(krope_list, axis=1)
    new_state_delta = jnp.stack(state_list, axis=1).astype(x_prompt.dtype)
    return (yp, ys, new_cache_ckv, new_cache_krope, new_state_delta)
```

```python
import functools

import jax
import jax.numpy as jnp
from jax import lax
from jax.experimental import pallas as pl
from jax.experimental.pallas import tpu as pltpu

F32 = jnp.float32
BF16 = jnp.bfloat16

D = 2048
DEPTH = 4
GRID_W = 64
H = 8
DK = 128
CONV_W = 5
CHUNK = 64
Q_LORA = 512
KV_LORA = 256
ROPE = 64
QK_DIM = DK + ROPE
ROPE_BASE = 10000.0
N_EXP = 16
D_FF = 512
N_MOD = 6
EPS = 1e-6
NEG = -1e30

LANE = 128
QKVZ = 4 * H * DK
TAIL = 1024
PROJ = QKVZ + TAIL
HQ = 2 * LANE

VMEM_LIMIT = 56 * 1024 * 1024


def _cp(*sem):
    return pltpu.CompilerParams(dimension_semantics=sem, vmem_limit_bytes=VMEM_LIMIT)


def _sigmoid(x):
    return 1.0 / (1.0 + jnp.exp(-x))


def _softplus(x):
    return jnp.maximum(x, 0.0) + jnp.log(1.0 + jnp.exp(-jnp.abs(x)))


def _mm(a, b):
    return jnp.dot(a.astype(BF16), b.astype(BF16), preferred_element_type=F32)


def _mm_nt(a, b):
    return lax.dot_general(a.astype(BF16), b.astype(BF16), (((1,), (1,)), ((), ())), preferred_element_type=F32)


def _mm_tn(a, b):
    return lax.dot_general(a.astype(BF16), b.astype(BF16), (((0,), (0,)), ((), ())), preferred_element_type=F32)


def _mm_f32(a, b):
    return jnp.dot(a, b, preferred_element_type=F32, precision=lax.Precision.HIGHEST)


def _mod_body(c_ref, w_ref, b_ref, o_ref):
    c = c_ref[...]
    o_ref[...] = _mm(c * _sigmoid(c), w_ref[...]) + b_ref[...]


def _modulation(conds, w_mod, b_mod):
    tn = 1024
    out = pl.pallas_call(
        _mod_body,
        grid=(DEPTH, N_MOD * D // tn),
        in_specs=[
            pl.BlockSpec((8, D), lambda l, j: (0, 0)),
            pl.BlockSpec((None, D, tn), lambda l, j: (l, 0, j)),
            pl.BlockSpec((None, 1, tn), lambda l, j: (l, 0, j)),
        ],
        out_specs=pl.BlockSpec((None, 8, tn), lambda l, j: (l, 0, j)),
        out_shape=jax.ShapeDtypeStruct((DEPTH, 8, N_MOD * D), F32),
        compiler_params=_cp("arbitrary", "arbitrary"),
        name="modulation",
    )(conds, w_mod, b_mod.reshape(DEPTH, 1, N_MOD * D))
    return out.reshape(DEPTH, 8, N_MOD, D)


def _in_body(x_ref, m_ref, g_ref, wm_ref, wt_ref, o_ref, h_scr, *, n_main):
    j = pl.program_id(1)

    @pl.when(j == 0)
    def _():
        x = x_ref[...]
        r = lax.rsqrt(jnp.mean(x * x, axis=-1, keepdims=True) + EPS)
        m = m_ref[...]
        h_scr[...] = (x * r * g_ref[...] * (1.0 + m[1:2, :]) + m[0:1, :]).astype(BF16)

    @pl.when(j < n_main)
    def _():
        o_ref[...] = jnp.dot(h_scr[...], wm_ref[...].astype(BF16), preferred_element_type=F32)

    @pl.when(j >= n_main)
    def _():
        o_ref[...] = jnp.dot(h_scr[...], wt_ref[...], preferred_element_type=F32)


def _in_proj(x, mods, g_mix, w_in, w_tail, l, cond_of_row):
    n = x.shape[0]
    tm, tn = 1024, 512
    n_main = QKVZ // tn
    n_tail = TAIL // tn
    return pl.pallas_call(
        functools.partial(_in_body, n_main=n_main),
        grid=(n // tm, n_main + n_tail),
        in_specs=[
            pl.BlockSpec((tm, D), lambda i, j: (i, 0)),
            pl.BlockSpec((None, None, N_MOD, D), lambda i, j: (l, cond_of_row(i * tm), 0, 0)),
            pl.BlockSpec((None, 1, D), lambda i, j: (l, 0, 0)),
            pl.BlockSpec((None, D, tn), lambda i, j: (l, 0, jnp.minimum(j, n_main - 1))),
            pl.BlockSpec((None, D, tn), lambda i, j: (l, 0, jnp.maximum(j - n_main, 0))),
        ],
        out_specs=pl.BlockSpec((tm, tn), lambda i, j: (i, j)),
        out_shape=jax.ShapeDtypeStruct((n, PROJ), F32),
        scratch_shapes=[pltpu.VMEM((tm, D), BF16)],
        compiler_params=_cp("arbitrary", "arbitrary"),
        name="in_proj",
    )(x, mods, g_mix.reshape(DEPTH, 1, D), w_in, w_tail)


def _unit_triangular_inverse(lmat, ii, jj):
    b16 = (ii >> 4) == (jj >> 4)
    b32 = (ii >> 5) == (jj >> 5)
    eye = jnp.where(ii == jj, 1.0, 0.0)
    ld = jnp.where(b16, lmat, 0.0)
    l1 = jnp.where(b32, lmat - ld, 0.0)
    l2 = jnp.where(b32, 0.0, lmat)
    p = eye - ld
    a = _mm_f32(ld, ld)
    p = p + _mm_f32(p, a)
    a = _mm_f32(a, a)
    p = p + _mm_f32(p, a)
    a = _mm_f32(a, a)
    p = p + _mm_f32(p, a)
    t32 = p - _mm_f32(_mm_f32(p, l1), p)
    return t32 - _mm_f32(_mm_f32(t32, l2), t32)


def _dn_body(q_ref, k_ref, v_ref, z_ref, ab_ref, cwq_ref, cwk_ref, cwv_ref, alog_ref, dtb_ref, gout_ref, s0_ref,
             o_ref, sfin_ref, q_s, k_s, v_s, gc_s, beta_s, of_s, ob_s, *, t):
    n_chunks = t // CHUNK
    row = lax.broadcasted_iota(jnp.int32, (t, LANE), 0)
    lane = lax.broadcasted_iota(jnp.int32, (t, LANE), 1)
    head = pl.program_id(1)

    def conv_act(x_ref, cw_ref):
        x = x_ref[...]
        cw = cw_ref[...]
        acc = x * cw[2:3, :]
        for j in (0, 1, 3, 4):
            d = j - CONV_W // 2
            shifted = pltpu.roll(x, (-d) % t, axis=0)
            ok = jnp.logical_and(row + d >= 0, row + d < t)
            acc = acc + jnp.where(ok, shifted, 0.0) * cw[j:j + 1, :]
        return acc * _sigmoid(acc)

    def l2n(x):
        return x * lax.rsqrt(jnp.sum(x * x, axis=-1, keepdims=True) + EPS)

    q_s[...] = l2n(conv_act(q_ref, cwq_ref)) * (DK ** -0.5)
    k_s[...] = l2n(conv_act(k_ref, cwk_ref))
    v_s[...] = conv_act(v_ref, cwv_ref)

    ab = ab_ref[...]
    g_all = -jnp.exp(alog_ref[...]) * _softplus(ab + dtb_ref[...])
    beta_all = _sigmoid(ab)
    pre = g_all
    suf = g_all
    r_in = row & (CHUNK - 1)
    s = 1
    while s < CHUNK:
        pre = pre + jnp.where(r_in >= s, pltpu.roll(pre, s, axis=0), 0.0)
        suf = suf + jnp.where(r_in < CHUNK - s, pltpu.roll(suf, t - s, axis=0), 0.0)
        s *= 2

    def column(x, idx):
        c = jnp.sum(jnp.where(lane == idx, x, 0.0), axis=1, keepdims=True)
        return jnp.broadcast_to(c, (t, LANE))

    gc_s[0] = column(pre, head)
    gc_s[1] = column(suf, H + head)
    beta_s[0] = column(beta_all, 2 * H + head)
    beta_s[1] = column(beta_all, 3 * H + head)

    ii = lax.broadcasted_iota(jnp.int32, (CHUNK, CHUNK), 0)
    jj = lax.broadcasted_iota(jnp.int32, (CHUNK, CHUNK), 1)

    def chunk_step(c, d, state, out_s):
        rows = pl.ds(pl.multiple_of(c * CHUNK, CHUNK), CHUNK)
        q = q_s[rows, :]
        k = k_s[rows, :]
        v = v_s[rows, :]
        gc = gc_s[d, rows, :]
        beta = beta_s[d, rows, :]
        gc_c = gc[:, :CHUNK]
        gc_row = jnp.sum(jnp.where(ii == jj, gc_c, 0.0), axis=0, keepdims=True)
        if d == 0:
            incl, strict = ii >= jj, ii > jj
            g_end = gc[CHUNK - 1:CHUNK, :]
        else:
            incl, strict = ii <= jj, ii < jj
            g_end = gc[0:1, :]
        decay = jnp.exp(jnp.where(incl, gc_c - gc_row, NEG))
        kb = k * beta
        lmat = jnp.where(strict, _mm_nt(kb, k) * decay, 0.0)
        tmat = _unit_triangular_inverse(lmat, ii, jj)
        e_gc = jnp.exp(gc)
        u = _mm(tmat, v * beta)
        w = _mm(tmat, kb * e_gc)
        attn = jnp.where(incl, _mm_nt(q, k) * decay, 0.0)
        v_new = u - _mm(w, state)
        out_s[rows, :] = _mm(q * e_gc, state) + _mm(attn, v_new)
        return state * jnp.exp(g_end) + _mm_tn(k * jnp.exp(g_end - gc), v_new)

    def body(i, carry):
        sf, sb = carry
        sf = chunk_step(i, 0, sf, of_s)
        sb = chunk_step(n_chunks - 1 - i, 1, sb, ob_s)
        return sf, sb

    sf, sb = lax.fori_loop(0, n_chunks, body, (s0_ref[0], s0_ref[1]))
    sfin_ref[0] = sf
    sfin_ref[1] = sb

    o = of_s[...] + ob_s[...]
    y = o * lax.rsqrt(jnp.mean(o * o, axis=-1, keepdims=True) + EPS) * gout_ref[...]
    z = z_ref[...]
    o_ref[...] = (y * z * _sigmoid(z)).astype(BF16)


def _deltanet(proj, conv_w, alog, dtb, g_out, s0, l, t, row0):
    nseq = s0.shape[0]
    rb0 = row0 // t
    tok = lambda off: pl.BlockSpec((t, LANE), lambda s, h: (rb0 + s, off + h))
    cw = lambda off: pl.BlockSpec((None, CONV_W, LANE), lambda s, h: (l, 0, off + h))
    small = pl.BlockSpec((None, 1, LANE), lambda s, h: (l, 0, 0))
    st = pl.BlockSpec((None, 2, None, DK, DK), lambda s, h: (s, 0, h, 0, 0))
    return pl.pallas_call(
        functools.partial(_dn_body, t=t),
        grid=(nseq, H),
        in_specs=[tok(0), tok(H), tok(2 * H), tok(3 * H),
                  pl.BlockSpec((t, LANE), lambda s, h: (rb0 + s, PROJ // LANE - 1)),
                  cw(0), cw(H), cw(2 * H), small, small, small, st],
        out_specs=[pl.BlockSpec((t, LANE), lambda s, h: (s, h)), st],
        out_shape=[jax.ShapeDtypeStruct((nseq * t, H * DK), BF16),
                   jax.ShapeDtypeStruct((nseq, 2, H, DK, DK), F32)],
        scratch_shapes=[pltpu.VMEM((t, LANE), F32), pltpu.VMEM((t, LANE), F32), pltpu.VMEM((t, LANE), F32),
                        pltpu.VMEM((2, t, LANE), F32), pltpu.VMEM((2, t, LANE), F32),
                        pltpu.VMEM((t, LANE), F32), pltpu.VMEM((t, LANE), F32)],
        compiler_params=_cp("arbitrary", "arbitrary"),
        name="deltanet",
    )(proj, proj, proj, proj, proj, conv_w, conv_w, conv_w, alog, dtb, g_out.reshape(DEPTH, 1, DK), s0)


def _rope_swap(x, lane):
    return jnp.where((lane & 31) < 16, pltpu.roll(x, LANE - 16, axis=1), pltpu.roll(x, 16, axis=1))


def _head_keys_values(kv, kr, gk, cos, sin, lane, k_ref, v_ref):
    kr_ss = jnp.sum(kr * kr, axis=-1, keepdims=True)
    for h in range(H):
        kn = kv[:, h * HQ:h * HQ + DK]
        r = lax.rsqrt((jnp.sum(kn * kn, axis=-1, keepdims=True) + kr_ss) * (1.0 / QK_DIM) + EPS)
        k_ref[:, h * HQ:h * HQ + DK] = (kn * r * gk[:, :DK]).astype(BF16)
        rr = kr * r * gk[:, DK:]
        if cos is not None:
            rr = rr * cos + _rope_swap(rr, lane) * sin
        k_ref[:, h * HQ + DK:(h + 1) * HQ] = rr.astype(BF16)
        v_ref[:, h * DK:(h + 1) * DK] = kv[:, h * HQ + DK:(h + 1) * HQ].astype(BF16)


def _mla_prep_body(cq_ref, ckv_ref, kr_ref, wq_ref, wkv_ref, gqa_ref, gkva_ref, gq_ref, gk_ref, cos_ref, sin_ref,
                   q_ref, k_ref, v_ref, ckvn_ref, krope_ref):
    tm = cq_ref.shape[0]
    lane = lax.broadcasted_iota(jnp.int32, (tm, LANE), 1)
    cos = cos_ref[...]
    sin = sin_ref[...]
    cq = cq_ref[...]
    cqn = cq * lax.rsqrt(jnp.mean(cq * cq, axis=-1, keepdims=True) + EPS) * gqa_ref[...]
    q = jnp.dot(cqn.astype(BF16), wq_ref[...], preferred_element_type=F32)
    gq = gq_ref[...]
    for h in range(H):
        qn = q[:, h * HQ:h * HQ + DK]
        qr = q[:, h * HQ + DK:(h + 1) * HQ]
        ss = jnp.sum(qn * qn, axis=-1, keepdims=True) + jnp.sum(qr * qr, axis=-1, keepdims=True)
        r = lax.rsqrt(ss * (1.0 / QK_DIM) + EPS) * (QK_DIM ** -0.5)
        q_ref[:, h * HQ:h * HQ + DK] = (qn * r * gq[:, :DK]).astype(BF16)
        qr = qr * r * gq[:, DK:]
        q_ref[:, h * HQ + DK:(h + 1) * HQ] = (qr * cos + _rope_swap(qr, lane) * sin).astype(BF16)
    ckv = ckv_ref[...]
    ckvn = ckv * lax.rsqrt(jnp.mean(ckv * ckv, axis=-1, keepdims=True) + EPS) * gkva_ref[...]
    ckvn_ref[...] = ckvn
    kr = kr_ref[...]
    krope_ref[...] = kr[:, :ROPE]
    kv = jnp.dot(ckvn.astype(BF16), wkv_ref[...], preferred_element_type=F32)
    _head_keys_values(kv, kr, gk_ref[...], cos, sin, lane, k_ref, v_ref)


def _mla_prep(proj, wq, wkv, g_qa, g_kva, gq, gk, cos_t, sin_t, l, pos_block):
    n = proj.shape[0]
    tm = 512
    lw = lambda shape: pl.BlockSpec((None,) + shape, lambda i: (l,) + (0,) * len(shape))
    table = pl.BlockSpec((tm, LANE), lambda i: (pos_block(i * tm, tm), 0))
    return pl.pallas_call(
        _mla_prep_body,
        grid=(n // tm,),
        in_specs=[pl.BlockSpec((tm, Q_LORA), lambda i: (i, QKVZ // Q_LORA)),
                  pl.BlockSpec((tm, KV_LORA), lambda i: (i, (QKVZ + Q_LORA) // KV_LORA)),
                  pl.BlockSpec((tm, LANE), lambda i: (i, (QKVZ + Q_LORA + KV_LORA) // LANE)),
                  lw((Q_LORA, H * HQ)), lw((KV_LORA, H * HQ)), lw((1, Q_LORA)), lw((1, KV_LORA)),
                  lw((1, HQ)), lw((1, HQ)), table, table],
        out_specs=[pl.BlockSpec((tm, H * HQ), lambda i: (i, 0)), pl.BlockSpec((tm, H * HQ), lambda i: (i, 0)),
                   pl.BlockSpec((tm, H * DK), lambda i: (i, 0)), pl.BlockSpec((tm, KV_LORA), lambda i: (i, 0)),
                   pl.BlockSpec((tm, ROPE), lambda i: (i, 0))],
        out_shape=[jax.ShapeDtypeStruct((n, H * HQ), BF16), jax.ShapeDtypeStruct((n, H * HQ), BF16),
                   jax.ShapeDtypeStruct((n, H * DK), BF16), jax.ShapeDtypeStruct((n, KV_LORA), F32),
                   jax.ShapeDtypeStruct((n, ROPE), F32)],
        compiler_params=_cp("arbitrary"),
        name="mla_prep",
    )(proj, proj, proj, wq, wkv, g_qa, g_kva, gq, gk, cos_t, sin_t)


def _cache_prep_body(ckv_ref, kr_ref, wkv_ref, gk_ref, k_ref, v_ref):
    kv = jnp.dot(ckv_ref[...].astype(BF16), wkv_ref[...], preferred_element_type=F32)
    _head_keys_values(kv, kr_ref[...], gk_ref[...], None, None, None, k_ref, v_ref)


def _cache_prep(cache_ckv, cache_kr, wkv, gk, l):
    nb, _, past, _ = cache_ckv.shape
    return pl.pallas_call(
        _cache_prep_body,
        grid=(nb,),
        in_specs=[pl.BlockSpec((None, None, past, KV_LORA), lambda b: (b, l, 0, 0)),
                  pl.BlockSpec((None, None, past, LANE), lambda b: (b, l, 0, 0)),
                  pl.BlockSpec((None, KV_LORA, H * HQ), lambda b: (l, 0, 0)),
                  pl.BlockSpec((None, 1, HQ), lambda b: (l, 0, 0))],
        out_specs=[pl.BlockSpec((past, H * HQ), lambda b: (b, 0)), pl.BlockSpec((past, H * DK), lambda b: (b, 0))],
        out_shape=[jax.ShapeDtypeStruct((nb * past, H * HQ), BF16), jax.ShapeDtypeStruct((nb * past, H * DK), BF16)],
        compiler_params=_cp("arbitrary"),
        name="cache_prep",
    )(cache_ckv, cache_kr, wkv, gk)


def _attn_body(*refs, nparts):
    q = refs[0][...]
    k_refs = refs[1:1 + nparts]
    v_refs = refs[1 + nparts:1 + 2 * nparts]
    o_ref = refs[-1]
    scores = [lax.dot_general(q, k[...], (((1,), (1,)), ((), ())), preferred_element_type=F32) for k in k_refs]
    m = jnp.max(scores[0], axis=-1, keepdims=True)
    for sc in scores[1:]:
        m = jnp.maximum(m, jnp.max(sc, axis=-1, keepdims=True))
    num = 0.0
    den = 0.0
    for sc, v in zip(scores, v_refs):
        p = jnp.exp(sc - m)
        den = den + jnp.sum(p, axis=-1, keepdims=True)
        num = num + jnp.dot(p.astype(BF16), v[...], preferred_element_type=F32)
    o_ref[...] = (num / den).astype(BF16)


def _attention(q, parts, nseq, t, row0, tq):
    nparts = len(parts)
    qb0 = row0 // tq
    nq = t // tq
    k_specs, v_specs, ks, vs = [], [], [], []
    for k_arr, v_arr, s_len, k_row0 in parts:
        kb0 = k_row0 // s_len
        k_specs.append(pl.BlockSpec((s_len, HQ), lambda s, h, i, kb0=kb0: (kb0 + s, h)))
        v_specs.append(pl.BlockSpec((s_len, DK), lambda s, h, i, kb0=kb0: (kb0 + s, h)))
        ks.append(k_arr)
        vs.append(v_arr)
    return pl.pallas_call(
        functools.partial(_attn_body, nparts=nparts),
        grid=(nseq, H, nq),
        in_specs=[pl.BlockSpec((tq, HQ), lambda s, h, i: (qb0 + s * nq + i, h))] + k_specs + v_specs,
        out_specs=pl.BlockSpec((tq, DK), lambda s, h, i: (s * nq + i, h)),
        out_shape=jax.ShapeDtypeStruct((nseq * t, H * DK), BF16),
        compiler_params=_cp("arbitrary", "arbitrary", "arbitrary"),
        name="attention",
    )(q, *ks, *vs)


def _out_body(x_ref, dn_ref, mla_ref, w_ref, m_ref, o_ref):
    w = w_ref[...]
    y = jnp.dot(dn_ref[...], w[:H * DK, :], preferred_element_type=F32)
    y = y + jnp.dot(mla_ref[...], w[H * DK:, :], preferred_element_type=F32)
    o_ref[...] = x_ref[...] + m_ref[2:3, :] * y


def _out_proj(x, dn, mla, w_out, mods, l, cond_of_row):
    n = x.shape[0]
    tm, tn = 512, 512
    return pl.pallas_call(
        _out_body,
        grid=(n // tm, D // tn),
        in_specs=[pl.BlockSpec((tm, tn), lambda i, j: (i, j)),
                  pl.BlockSpec((tm, H * DK), lambda i, j: (i, 0)),
                  pl.BlockSpec((tm, H * DK), lambda i, j: (i, 0)),
                  pl.BlockSpec((None, 2 * H * DK, tn), lambda i, j: (l, 0, j)),
                  pl.BlockSpec((None, None, N_MOD, tn), lambda i, j: (l, cond_of_row(i * tm), 0, j))],
        out_specs=pl.BlockSpec((tm, tn), lambda i, j: (i, j)),
        out_shape=jax.ShapeDtypeStruct((n, D), F32),
        compiler_params=_cp("arbitrary", "arbitrary"),
        name="out_proj",
    )(x, dn, mla, w_out, mods)


def _group_lane(x, k, lane):
    return jnp.where((lane & 3) + k < 4, pltpu.roll(x, LANE - k, axis=1), pltpu.roll(x, 4 - k, axis=1))


def _route_body(x_ref, m_ref, g_ref, wr_ref, br_ref, h_ref, comb_ref):
    x = x_ref[...]
    m = m_ref[...]
    h = x * lax.rsqrt(jnp.mean(x * x, axis=-1, keepdims=True) + EPS) * g_ref[...] * (1.0 + m[4:5, :]) + m[3:4, :]
    h_ref[...] = h.astype(BF16)
    tm = x.shape[0]
    lane = lax.broadcasted_iota(jnp.int32, (tm, LANE), 1)
    valid = lane < N_EXP
    scores = _sigmoid(_mm_f32(h, wr_ref[...]))
    sel = jnp.where(valid, scores + br_ref[...], NEG)
    rank = jnp.zeros((tm, LANE), F32)
    for k in (1, 2, 3):
        other = _group_lane(sel, k, lane)
        other_first = (lane & 3) + k >= 4
        beats = jnp.logical_or(other > sel, jnp.logical_and(other == sel, other_first))
        rank = rank + jnp.where(beats, 1.0, 0.0)
    top2 = rank < 2.0
    t = jnp.where(top2, sel, 0.0)
    gscore = t + _group_lane(t, 1, lane) + _group_lane(t, 2, lane) + _group_lane(t, 3, lane)
    lost = jnp.zeros((tm, LANE), F32)
    for k in (1, 2, 3):
        wrapped = lane + 4 * k >= N_EXP
        other = jnp.where(wrapped, pltpu.roll(gscore, N_EXP - 4 * k, axis=1), pltpu.roll(gscore, LANE - 4 * k, axis=1))
        loses = jnp.logical_or(other > gscore, jnp.logical_and(other == gscore, wrapped))
        lost = lost + jnp.where(loses, 1.0, 0.0)
    chosen = jnp.logical_and(jnp.logical_and(lost == 0.0, top2), valid)
    num = jnp.where(chosen, scores, 0.0)
    comb_ref[...] = num / jnp.sum(num, axis=-1, keepdims=True)


def _route(x, mods, g_ffn, w_router, b_router, l, cond_of_row):
    n = x.shape[0]
    tm = 512
    return pl.pallas_call(
        _route_body,
        grid=(n // tm,),
        in_specs=[pl.BlockSpec((tm, D), lambda i: (i, 0)),
                  pl.BlockSpec((None, None, N_MOD, D), lambda i: (l, cond_of_row(i * tm), 0, 0)),
                  pl.BlockSpec((None, 1, D), lambda i: (l, 0, 0)),
                  pl.BlockSpec((D, LANE), lambda i: (0, 0)),
                  pl.BlockSpec((1, LANE), lambda i: (0, 0))],
        out_specs=[pl.BlockSpec((tm, D), lambda i: (i, 0)), pl.BlockSpec((tm, LANE), lambda i: (i, 0))],
        out_shape=[jax.ShapeDtypeStruct((n, D), BF16), jax.ShapeDtypeStruct((n, LANE), F32)],
        compiler_params=_cp("arbitrary"),
        name="route",
    )(x, mods, g_ffn.reshape(DEPTH, 1, D), w_router, b_router)


def _moe_body(h_ref, comb_ref, wg_ref, wu_ref, wd_ref, x_ref, m_ref, o_ref, acc_s):
    e = pl.program_id(1)

    @pl.when(e == 0)
    def _():
        acc_s[...] = jnp.zeros_like(acc_s)

    h = h_ref[...]
    comb = comb_ref[...]
    lane = lax.broadcasted_iota(jnp.int32, comb.shape, 1)
    ce = jnp.sum(jnp.where(lane == e, comb, 0.0), axis=1, keepdims=True)
    hg = jnp.dot(h, wg_ref[...], preferred_element_type=F32)
    hu = jnp.dot(h, wu_ref[...], preferred_element_type=F32)
    act = hg * _sigmoid(hg) * hu * ce
    acc_s[...] += jnp.dot(act.astype(BF16), wd_ref[...], preferred_element_type=F32)

    @pl.when(e == N_EXP - 1)
    def _():
        o_ref[...] = x_ref[...] + m_ref[5:6, :] * acc_s[...]


def _moe(h, comb, w_gate, w_up, w_down, x, mods, l, cond_of_row):
    n = x.shape[0]
    tm = 512
    return pl.pallas_call(
        _moe_body,
        grid=(n // tm, N_EXP),
        in_specs=[pl.BlockSpec((tm, D), lambda i, e: (i, 0)),
                  pl.BlockSpec((tm, LANE), lambda i, e: (i, 0)),
                  pl.BlockSpec((None, None, D, D_FF), lambda i, e: (l, e, 0, 0)),
                  pl.BlockSpec((None, None, D, D_FF), lambda i, e: (l, e, 0, 0)),
                  pl.BlockSpec((None, None, D_FF, D), lambda i, e: (l, e, 0, 0)),
                  pl.BlockSpec((tm, D), lambda i, e: (i, 0)),
                  pl.BlockSpec((None, None, N_MOD, D), lambda i, e: (l, cond_of_row(i * tm), 0, 0))],
        out_specs=pl.BlockSpec((tm, D), lambda i, e: (i, 0)),
        out_shape=jax.ShapeDtypeStruct((n, D), F32),
        scratch_shapes=[pltpu.VMEM((tm, D), F32)],
        compiler_params=_cp("arbitrary", "arbitrary"),
        name="experts",
    )(h, comb, w_gate, w_up, w_down, x, mods)


def _rope_tables(n_pos, n_identity):
    t = jnp.arange(n_pos)
    quarter = ROPE // 4
    inv = ROPE_BASE ** (-jnp.arange(quarter, dtype=F32) / quarter)
    ang_r = (t // GRID_W).astype(F32)[:, None] * inv
    ang_c = (t % GRID_W).astype(F32)[:, None] * inv
    one = jnp.ones((n_pos, LANE - ROPE), F32)
    cos = jnp.concatenate([jnp.cos(ang_r), jnp.cos(ang_r), jnp.cos(ang_c), jnp.cos(ang_c), one], axis=1)
    sin = jnp.concatenate([-jnp.sin(ang_r), jnp.sin(ang_r), -jnp.sin(ang_c), jnp.sin(ang_c), 0.0 * one], axis=1)
    cos = jnp.concatenate([cos, jnp.ones((n_identity, LANE), F32)], axis=0)
    sin = jnp.concatenate([sin, jnp.zeros((n_identity, LANE), F32)], axis=0)
    return cos, sin


def _pad_heads(w):
    lead = w.shape[:-1]
    w = w.reshape(lead + (H, QK_DIM))
    w = jnp.pad(w, [(0, 0)] * len(lead) + [(0, 0), (0, HQ - QK_DIM)])
    return w.reshape(lead + (H * HQ,))


def kernel(x_prompt, x_sample, cache_ckv, cache_krope, state_delta, c, c_ctx, g_mix, w_mod, b_mod, w_in, conv_w,
           a_log, dt_bias, g_dn_out, g_qa, w_uq, g_kva, w_ukv, g_qh, g_kh, w_out, g_ffn, w_router, b_router,
           w_gate, w_up, w_down):
    nb, seq, _ = x_prompt.shape
    ndb, dseq, _ = x_sample.shape
    n_ctx = nb * seq
    past = cache_ckv.shape[2]

    def cond_of_row(r):
        return jnp.where(r < n_ctx, 0, 1 + (r - n_ctx) // dseq)

    def pos_block(r, tm):
        return jnp.where(r < n_ctx, dseq // tm, ((r - n_ctx) % dseq) // tm)

    conds = jnp.concatenate([c_ctx[None, :], c, jnp.zeros((8 - 1 - ndb, D), F32)], axis=0)
    o_z = QKVZ
    o_a, o_b = o_z, o_z + 2 * H
    o_cq = o_b + 2 * H
    o_ckv = o_cq + Q_LORA
    o_kr = o_ckv + KV_LORA
    zc = lambda k: jnp.zeros((DEPTH, D, k), F32)
    w_tail = jnp.concatenate([w_in[:, :, o_cq:o_ckv], w_in[:, :, o_ckv:o_kr], w_in[:, :, o_kr:o_kr + ROPE],
                              zc(LANE - ROPE), w_in[:, :, o_a:o_cq], zc(LANE - 4 * H)], axis=2).astype(BF16)
    pad_lane = lambda v: jnp.pad(v.reshape(DEPTH, 1, -1), ((0, 0), (0, 0), (0, LANE - 2 * H)))
    alog_p = pad_lane(a_log)
    dtb_p = pad_lane(dt_bias)
    wq_p = _pad_heads(w_uq).astype(BF16)
    wkv_p = w_ukv.astype(BF16)
    gq_p = jnp.pad(g_qh, ((0, 0), (0, HQ - QK_DIM))).reshape(DEPTH, 1, HQ)
    gk_p = jnp.pad(g_kh, ((0, 0), (0, HQ - QK_DIM))).reshape(DEPTH, 1, HQ)
    g_qa_p = g_qa.reshape(DEPTH, 1, Q_LORA)
    g_kva_p = g_kva.reshape(DEPTH, 1, KV_LORA)
    cache_kr_p = jnp.pad(cache_krope, ((0, 0), (0, 0), (0, 0), (0, LANE - ROPE)))
    w_out_b = w_out.astype(BF16)
    wr_p = jnp.pad(w_router, ((0, 0), (0, LANE - N_EXP)))
    br_p = jnp.pad(b_router, (0, LANE - N_EXP)).reshape(1, LANE)
    w_gate_b, w_up_b, w_down_b = w_gate.astype(BF16), w_up.astype(BF16), w_down.astype(BF16)
    cos_t, sin_t = _rope_tables(dseq, 512)
    s0_ctx = jnp.zeros((nb, 2, H, DK, DK), F32)

    mods = _modulation(conds, w_mod, b_mod)
    x = jnp.concatenate([x_prompt.reshape(n_ctx, D), x_sample.reshape(ndb * dseq, D)], axis=0)

    ckv_list, krope_list, state_list = [], [], []
    for l in range(DEPTH):
        proj = _in_proj(x, mods, g_mix, w_in, w_tail, l, cond_of_row)
        dn_ctx, s_ctx = _deltanet(proj, conv_w, alog_p, dtb_p, g_dn_out, s0_ctx, l, seq, 0)
        dn_lat, _ = _deltanet(proj, conv_w, alog_p, dtb_p, g_dn_out, state_delta[:, l], l, dseq, n_ctx)
        q, k, v, ckvn, krope = _mla_prep(proj, wq_p, wkv_p, g_qa_p, g_kva_p, gq_p, gk_p, cos_t, sin_t, l, pos_block)
        kc, vc = _cache_prep(cache_ckv, cache_kr_p, wkv_p, gk_p, l)
        mla_ctx = _attention(q, [(k, v, seq, 0)], nb, seq, 0, seq)
        mla_lat = _attention(q, [(kc, vc, past, 0), (k, v, dseq, n_ctx)], ndb, dseq, n_ctx, 512)
        dn = jnp.concatenate([dn_ctx, dn_lat], axis=0)
        mla = jnp.concatenate([mla_ctx, mla_lat], axis=0)
        x = _out_proj(x, dn, mla, w_out_b, mods, l, cond_of_row)
        h, comb = _route(x, mods, g_ffn, wr_p, br_p, l, cond_of_row)
        x = _moe(h, comb, w_gate_b, w_up_b, w_down_b, x, mods, l, cond_of_row)
        ckv_list.append(ckvn[:n_ctx].reshape(nb, seq, KV_LORA))
        krope_list.append(krope[:n_ctx].reshape(nb, seq, ROPE))
        state_list.append(s_ctx)

    y_prompt = x[:n_ctx].reshape(nb, seq, D)
    y_sample = x[n_ctx:].reshape(ndb, dseq, D)
    return (y_prompt, y_sample, jnp.stack(ckv_list, axis=1), jnp.stack(krope_list, axis=1),
            jnp.stack(state_list, axis=1))
```

```python
import functools

import jax
import jax.numpy as jnp
from jax import lax
from jax.experimental import pallas as pl
from jax.experimental.pallas import tpu as pltpu

F32 = jnp.float32
BF16 = jnp.bfloat16

D = 2048
DEPTH = 4
GRID_W = 64
H = 8
DK = 128
CONV_W = 5
CHUNK = 64
Q_LORA = 512
KV_LORA = 256
ROPE = 64
QK_DIM = DK + ROPE
ROPE_BASE = 10000.0
N_EXP = 16
D_FF = 512
N_MOD = 6
EPS = 1e-6
NEG = -1e30

LANE = 128
QKVZ = 4 * H * DK
TAIL = 1024
PROJ = QKVZ + TAIL
HQ = 2 * LANE

VMEM_LIMIT = 56 * 1024 * 1024


def _cp(*sem):
    return pltpu.CompilerParams(dimension_semantics=sem, vmem_limit_bytes=VMEM_LIMIT)


def _sigmoid(x):
    return 1.0 / (1.0 + jnp.exp(-x))


def _softplus(x):
    return jnp.maximum(x, 0.0) + jnp.log(1.0 + jnp.exp(-jnp.abs(x)))


def _mm(a, b):
    return jnp.dot(a.astype(BF16), b.astype(BF16), preferred_element_type=F32)


def _mm_nt(a, b):
    return lax.dot_general(a.astype(BF16), b.astype(BF16), (((1,), (1,)), ((), ())), preferred_element_type=F32)


def _mm_tn(a, b):
    return lax.dot_general(a.astype(BF16), b.astype(BF16), (((0,), (0,)), ((), ())), preferred_element_type=F32)


def _mm_f32(a, b):
    return jnp.dot(a, b, preferred_element_type=F32, precision=lax.Precision.HIGHEST)


def _mod_body(c_ref, w_ref, b_ref, o_ref):
    c = c_ref[...]
    o_ref[...] = _mm(c * _sigmoid(c), w_ref[...]) + b_ref[...]


def _modulation(conds, w_mod, b_mod):
    tn = 1024
    out = pl.pallas_call(
        _mod_body,
        grid=(DEPTH, N_MOD * D // tn),
        in_specs=[
            pl.BlockSpec((8, D), lambda l, j: (0, 0)),
            pl.BlockSpec((None, D, tn), lambda l, j: (l, 0, j)),
            pl.BlockSpec((None, 1, tn), lambda l, j: (l, 0, j)),
        ],
        out_specs=pl.BlockSpec((None, 8, tn), lambda l, j: (l, 0, j)),
        out_shape=jax.ShapeDtypeStruct((DEPTH, 8, N_MOD * D), F32),
        compiler_params=_cp("arbitrary", "arbitrary"),
        name="modulation",
    )(conds, w_mod, b_mod.reshape(DEPTH, 1, N_MOD * D))
    return out.reshape(DEPTH, 8, N_MOD, D)


def _in_body(x_ref, m_ref, g_ref, wm_ref, wt_ref, o_ref, h_scr, *, n_main):
    j = pl.program_id(1)

    @pl.when(j == 0)
    def _():
        x = x_ref[...]
        r = lax.rsqrt(jnp.mean(x * x, axis=-1, keepdims=True) + EPS)
        m = m_ref[...]
        h_scr[...] = (x * r * g_ref[...] * (1.0 + m[1:2, :]) + m[0:1, :]).astype(BF16)

    @pl.when(j < n_main)
    def _():
        o_ref[...] = jnp.dot(h_scr[...], wm_ref[...].astype(BF16), preferred_element_type=F32)

    @pl.when(j >= n_main)
    def _():
        o_ref[...] = jnp.dot(h_scr[...], wt_ref[...], preferred_element_type=F32)


def _in_proj(x, mods, g_mix, w_in, w_tail, l, cond_of_row):
    n = x.shape[0]
    tm, tn = 1024, 512
    n_main = QKVZ // tn
    n_tail = TAIL // tn
    return pl.pallas_call(
        functools.partial(_in_body, n_main=n_main),
        grid=(n // tm, n_main + n_tail),
        in_specs=[
            pl.BlockSpec((tm, D), lambda i, j: (i, 0)),
            pl.BlockSpec((None, None, N_MOD, D), lambda i, j: (l, cond_of_row(i * tm), 0, 0)),
            pl.BlockSpec((None, 1, D), lambda i, j: (l, 0, 0)),
            pl.BlockSpec((None, D, tn), lambda i, j: (l, 0, jnp.minimum(j, n_main - 1))),
            pl.BlockSpec((None, D, tn), lambda i, j: (l, 0, jnp.maximum(j - n_main, 0))),
        ],
        out_specs=pl.BlockSpec((tm, tn), lambda i, j: (i, j)),
        out_shape=jax.ShapeDtypeStruct((n, PROJ), F32),
        scratch_shapes=[pltpu.VMEM((tm, D), BF16)],
        compiler_params=_cp("arbitrary", "arbitrary"),
        name="in_proj",
    )(x, mods, g_mix.reshape(DEPTH, 1, D), w_in, w_tail)


def _bmm(a, b):
    return jnp.einsum('bij,bjk->bik', a.astype(BF16), b.astype(BF16), preferred_element_type=F32)


def _bmm_nt(a, b):
    return jnp.einsum('bid,bjd->bij', a.astype(BF16), b.astype(BF16), preferred_element_type=F32)


def _bmm_tn(a, b):
    return jnp.einsum('bci,bcj->bij', a.astype(BF16), b.astype(BF16), preferred_element_type=F32)


def _unit_triangular_inverse(lmat, ii, jj):
    b16 = (ii >> 4) == (jj >> 4)
    b32 = (ii >> 5) == (jj >> 5)
    eye = jnp.where(ii == jj, 1.0, 0.0)
    ld = jnp.where(b16, lmat, 0.0)
    l1 = jnp.where(b32, lmat - ld, 0.0)
    l2 = jnp.where(b32, 0.0, lmat)
    p = eye - ld
    a = _bmm(ld, ld)
    p = p + _bmm(p, a)
    a = _bmm(a, a)
    p = p + _bmm(p, a)
    a = _bmm(a, a)
    p = p + _bmm(p, a)
    t32 = p - _bmm(_bmm(p, l1), p)
    return t32 - _bmm(_bmm(t32, l2), t32)


def _dn_body(q_ref, k_ref, v_ref, z_ref, ab_ref, cwq_ref, cwk_ref, cwv_ref, alog_ref, dtb_ref, gout_ref, s0_ref,
             o_ref, sfin_ref, q_s, k_s, v_s, gc_s, beta_s, u_s, w_s, qd_s, kd_s, at_s, o_s, *, t, nh, unroll):
    n_chunks = t // CHUNK
    span = unroll * CHUNK
    nb = nh * unroll
    row = lax.broadcasted_iota(jnp.int32, (t, LANE), 0)
    lane = lax.broadcasted_iota(jnp.int32, (t, LANE), 1)
    head0 = pl.program_id(1) * nh

    def conv_act(x, cw):
        acc = x * cw[2:3, :]
        for j in (0, 1, 3, 4):
            d = j - CONV_W // 2
            shifted = pltpu.roll(x, (-d) % t, axis=0)
            ok = jnp.logical_and(row + d >= 0, row + d < t)
            acc = acc + jnp.where(ok, shifted, 0.0) * cw[j:j + 1, :]
        return acc * _sigmoid(acc)

    def l2n(x):
        return x * lax.rsqrt(jnp.sum(x * x, axis=-1, keepdims=True) + EPS)

    ab = ab_ref[...]
    g_all = -jnp.exp(alog_ref[...]) * _softplus(ab + dtb_ref[...])
    beta_all = _sigmoid(ab)
    pre = g_all
    suf = g_all
    r_in = row & (CHUNK - 1)
    s = 1
    while s < CHUNK:
        pre = pre + jnp.where(r_in >= s, pltpu.roll(pre, s, axis=0), 0.0)
        suf = suf + jnp.where(r_in < CHUNK - s, pltpu.roll(suf, t - s, axis=0), 0.0)
        s *= 2

    def column(x, idx):
        c = jnp.sum(jnp.where(lane == idx, x, 0.0), axis=1, keepdims=True)
        return jnp.broadcast_to(c, (t, LANE))

    for h in range(nh):
        sl = slice(h * LANE, (h + 1) * LANE)
        q_s[h] = l2n(conv_act(q_ref[:, sl], cwq_ref[:, sl])) * (DK ** -0.5)
        k_s[h] = l2n(conv_act(k_ref[:, sl], cwk_ref[:, sl]))
        v_s[h] = conv_act(v_ref[:, sl], cwv_ref[:, sl])
        gc_s[0, h] = column(pre, head0 + h)
        gc_s[1, h] = column(suf, H + head0 + h)
        beta_s[0, h] = column(beta_all, 2 * H + head0 + h)
        beta_s[1, h] = column(beta_all, 3 * H + head0 + h)
    o_s[...] = jnp.zeros_like(o_s)

    ii = lax.broadcasted_iota(jnp.int32, (1, CHUNK, CHUNK), 1)
    jj = lax.broadcasted_iota(jnp.int32, (1, CHUNK, CHUNK), 2)
    causal = ((ii >= jj, ii > jj), (ii <= jj, ii < jj))

    def chunk_end_gate(gc, d):
        return gc[:, CHUNK - 1:CHUNK, :] if d == 0 else gc[:, 0:1, :]

    def prepare_group(g, carry):
        rows = pl.ds(pl.multiple_of(g * span, span), span)
        per_chunk = lambda x: x.reshape(nb, CHUNK, x.shape[-1])
        per_head = lambda x: x.reshape(nh, span, x.shape[-1])
        q = per_chunk(q_s[:, rows, :])
        k = per_chunk(k_s[:, rows, :])
        v = per_chunk(v_s[:, rows, :])
        kk = _bmm_nt(k, k)
        qk = _bmm_nt(q, k)
        lmats, rhs = [], []
        for d in (0, 1):
            incl, strict = causal[d]
            gc = per_chunk(gc_s[d, :, rows, :])
            beta = per_chunk(beta_s[d, :, rows, :])
            gc_c = gc[:, :, :CHUNK]
            gc_row = jnp.sum(jnp.where(ii == jj, gc_c, 0.0), axis=1, keepdims=True)
            decay = jnp.exp(jnp.where(incl, gc_c - gc_row, NEG))
            e_gc = jnp.exp(gc)
            lmats.append(jnp.where(strict, kk * beta[:, :, :CHUNK] * decay, 0.0))
            rhs.append(jnp.concatenate([v * beta, k * beta * e_gc], axis=-1))
            qd_s[d, :, rows, :] = per_head(q * e_gc).astype(BF16)
            kd_s[d, :, rows, :] = per_head(k * jnp.exp(chunk_end_gate(gc, d) - gc)).astype(BF16)
            at_s[d, :, rows, :] = per_head(jnp.where(incl, qk * decay, 0.0)).astype(BF16)
        tmat = _unit_triangular_inverse(jnp.concatenate(lmats, axis=0), ii, jj)
        uw = _bmm(tmat, jnp.concatenate(rhs, axis=0))
        for d in (0, 1):
            part = uw[d * nb:(d + 1) * nb]
            u_s[d, :, rows, :] = per_head(part[:, :, :LANE])
            w_s[d, :, rows, :] = per_head(part[:, :, LANE:]).astype(BF16)
        return carry

    lax.fori_loop(0, n_chunks // unroll, prepare_group, 0)

    def scan_step(i, states):
        rows = [pl.ds(pl.multiple_of(c * CHUNK, CHUNK), CHUNK) for c in (i, n_chunks - 1 - i)]
        r = [_bmm(jnp.concatenate([w_s[d, :, rows[d], :], qd_s[d, :, rows[d], :]], axis=1), states[d])
             for d in (0, 1)]
        v_new = [(u_s[d, :, rows[d], :] - r[d][:, :CHUNK, :]).astype(BF16) for d in (0, 1)]
        o_add = [r[d][:, CHUNK:, :] + _bmm(at_s[d, :, rows[d], :], v_new[d]) for d in (0, 1)]
        s_add = [_bmm_tn(kd_s[d, :, rows[d], :], v_new[d]) for d in (0, 1)]
        out = []
        for d in (0, 1):
            o_s[:, rows[d], :] += o_add[d]
            out.append(states[d] * jnp.exp(chunk_end_gate(gc_s[d, :, rows[d], :], d)) + s_add[d])
        return tuple(out)

    sf, sb = lax.fori_loop(0, n_chunks, scan_step, (s0_ref[0], s0_ref[1]))
    sfin_ref[0] = sf
    sfin_ref[1] = sb

    for h in range(nh):
        sl = slice(h * LANE, (h + 1) * LANE)
        o = o_s[h]
        y = o * lax.rsqrt(jnp.mean(o * o, axis=-1, keepdims=True) + EPS) * gout_ref[...]
        z = z_ref[:, sl]
        o_ref[:, sl] = (y * z * _sigmoid(z)).astype(BF16)


def _deltanet(proj, conv_w, alog, dtb, g_out, s0, l, t, row0, nh, unroll):
    nseq = s0.shape[0]
    rb0 = row0 // t
    w = nh * LANE
    ng = H // nh
    tok = lambda part: pl.BlockSpec((t, w), lambda s, g: (rb0 + s, part * ng + g))
    cw = lambda part: pl.BlockSpec((None, CONV_W, w), lambda s, g: (l, 0, part * ng + g))
    small = pl.BlockSpec((None, 1, LANE), lambda s, g: (l, 0, 0))
    st = pl.BlockSpec((None, 2, nh, DK, DK), lambda s, g: (s, 0, g, 0, 0))
    wide = lambda dt: pltpu.VMEM((2, nh, t, LANE), dt)
    return pl.pallas_call(
        functools.partial(_dn_body, t=t, nh=nh, unroll=unroll),
        grid=(nseq, ng),
        in_specs=[tok(0), tok(1), tok(2), tok(3),
                  pl.BlockSpec((t, LANE), lambda s, g: (rb0 + s, PROJ // LANE - 1)),
                  cw(0), cw(1), cw(2), small, small, small, st],
        out_specs=[pl.BlockSpec((t, w), lambda s, g: (s, g)), st],
        out_shape=[jax.ShapeDtypeStruct((nseq * t, H * DK), BF16),
                   jax.ShapeDtypeStruct((nseq, 2, H, DK, DK), F32)],
        scratch_shapes=[pltpu.VMEM((nh, t, LANE), F32), pltpu.VMEM((nh, t, LANE), F32), pltpu.VMEM((nh, t, LANE), F32),
                        wide(F32), wide(F32), wide(F32), wide(BF16), wide(BF16), wide(BF16),
                        pltpu.VMEM((2, nh, t, CHUNK), BF16), pltpu.VMEM((nh, t, LANE), F32)],
        compiler_params=_cp("arbitrary", "arbitrary"),
        name="deltanet",
    )(proj, proj, proj, proj, proj, conv_w, conv_w, conv_w, alog, dtb, g_out.reshape(DEPTH, 1, DK), s0)


def _rope_swap(x, lane):
    return jnp.where((lane & 31) < 16, pltpu.roll(x, LANE - 16, axis=1), pltpu.roll(x, 16, axis=1))


def _head_keys_values(kv, kr, gk, cos, sin, lane, k_ref, v_ref):
    kr_ss = jnp.sum(kr * kr, axis=-1, keepdims=True)
    for h in range(H):
        kn = kv[:, h * HQ:h * HQ + DK]
        r = lax.rsqrt((jnp.sum(kn * kn, axis=-1, keepdims=True) + kr_ss) * (1.0 / QK_DIM) + EPS)
        k_ref[:, h * HQ:h * HQ + DK] = (kn * r * gk[:, :DK]).astype(BF16)
        rr = kr * r * gk[:, DK:]
        if cos is not None:
            rr = rr * cos + _rope_swap(rr, lane) * sin
        k_ref[:, h * HQ + DK:(h + 1) * HQ] = rr.astype(BF16)
        v_ref[:, h * DK:(h + 1) * DK] = kv[:, h * HQ + DK:(h + 1) * HQ].astype(BF16)


def _mla_prep_body(cq_ref, ckv_ref, kr_ref, wq_ref, wkv_ref, gqa_ref, gkva_ref, gq_ref, gk_ref, cos_ref, sin_ref,
                   q_ref, k_ref, v_ref, ckvn_ref, krope_ref):
    tm = cq_ref.shape[0]
    lane = lax.broadcasted_iota(jnp.int32, (tm, LANE), 1)
    cos = cos_ref[...]
    sin = sin_ref[...]
    cq = cq_ref[...]
    cqn = cq * lax.rsqrt(jnp.mean(cq * cq, axis=-1, keepdims=True) + EPS) * gqa_ref[...]
    q = jnp.dot(cqn.astype(BF16), wq_ref[...], preferred_element_type=F32)
    gq = gq_ref[...]
    for h in range(H):
        qn = q[:, h * HQ:h * HQ + DK]
        qr = q[:, h * HQ + DK:(h + 1) * HQ]
        ss = jnp.sum(qn * qn, axis=-1, keepdims=True) + jnp.sum(qr * qr, axis=-1, keepdims=True)
        r = lax.rsqrt(ss * (1.0 / QK_DIM) + EPS) * (QK_DIM ** -0.5)
        q_ref[:, h * HQ:h * HQ + DK] = (qn * r * gq[:, :DK]).astype(BF16)
        qr = qr * r * gq[:, DK:]
        q_ref[:, h * HQ + DK:(h + 1) * HQ] = (qr * cos + _rope_swap(qr, lane) * sin).astype(BF16)
    ckv = ckv_ref[...]
    ckvn = ckv * lax.rsqrt(jnp.mean(ckv * ckv, axis=-1, keepdims=True) + EPS) * gkva_ref[...]
    ckvn_ref[...] = ckvn
    kr = kr_ref[...]
    krope_ref[...] = kr[:, :ROPE]
    kv = jnp.dot(ckvn.astype(BF16), wkv_ref[...], preferred_element_type=F32)
    _head_keys_values(kv, kr, gk_ref[...], cos, sin, lane, k_ref, v_ref)


def _mla_prep(proj, wq, wkv, g_qa, g_kva, gq, gk, cos_t, sin_t, l, pos_block):
    n = proj.shape[0]
    tm = 512
    lw = lambda shape: pl.BlockSpec((None,) + shape, lambda i: (l,) + (0,) * len(shape))
    table = pl.BlockSpec((tm, LANE), lambda i: (pos_block(i * tm, tm), 0))
    return pl.pallas_call(
        _mla_prep_body,
        grid=(n // tm,),
        in_specs=[pl.BlockSpec((tm, Q_LORA), lambda i: (i, QKVZ // Q_LORA)),
                  pl.BlockSpec((tm, KV_LORA), lambda i: (i, (QKVZ + Q_LORA) // KV_LORA)),
                  pl.BlockSpec((tm, LANE), lambda i: (i, (QKVZ + Q_LORA + KV_LORA) // LANE)),
                  lw((Q_LORA, H * HQ)), lw((KV_LORA, H * HQ)), lw((1, Q_LORA)), lw((1, KV_LORA)),
                  lw((1, HQ)), lw((1, HQ)), table, table],
        out_specs=[pl.BlockSpec((tm, H * HQ), lambda i: (i, 0)), pl.BlockSpec((tm, H * HQ), lambda i: (i, 0)),
                   pl.BlockSpec((tm, H * DK), lambda i: (i, 0)), pl.BlockSpec((tm, KV_LORA), lambda i: (i, 0)),
                   pl.BlockSpec((tm, ROPE), lambda i: (i, 0))],
        out_shape=[jax.ShapeDtypeStruct((n, H * HQ), BF16), jax.ShapeDtypeStruct((n, H * HQ), BF16),
                   jax.ShapeDtypeStruct((n, H * DK), BF16), jax.ShapeDtypeStruct((n, KV_LORA), F32),
                   jax.ShapeDtypeStruct((n, ROPE), F32)],
        compiler_params=_cp("arbitrary"),
        name="mla_prep",
    )(proj, proj, proj, wq, wkv, g_qa, g_kva, gq, gk, cos_t, sin_t)


def _cache_prep_body(ckv_ref, kr_ref, wkv_ref, gk_ref, k_ref, v_ref):
    kv = jnp.dot(ckv_ref[...].astype(BF16), wkv_ref[...], preferred_element_type=F32)
    _head_keys_values(kv, kr_ref[...], gk_ref[...], None, None, None, k_ref, v_ref)


def _cache_prep(cache_ckv, cache_kr, wkv, gk, l):
    nb, _, past, _ = cache_ckv.shape
    return pl.pallas_call(
        _cache_prep_body,
        grid=(nb,),
        in_specs=[pl.BlockSpec((None, None, past, KV_LORA), lambda b: (b, l, 0, 0)),
                  pl.BlockSpec((None, None, past, LANE), lambda b: (b, l, 0, 0)),
                  pl.BlockSpec((None, KV_LORA, H * HQ), lambda b: (l, 0, 0)),
                  pl.BlockSpec((None, 1, HQ), lambda b: (l, 0, 0))],
        out_specs=[pl.BlockSpec((past, H * HQ), lambda b: (b, 0)), pl.BlockSpec((past, H * DK), lambda b: (b, 0))],
        out_shape=[jax.ShapeDtypeStruct((nb * past, H * HQ), BF16), jax.ShapeDtypeStruct((nb * past, H * DK), BF16)],
        compiler_params=_cp("arbitrary"),
        name="cache_prep",
    )(cache_ckv, cache_kr, wkv, gk)


def _attn_body(*refs, nparts):
    q = refs[0][...]
    k_refs = refs[1:1 + nparts]
    v_refs = refs[1 + nparts:1 + 2 * nparts]
    o_ref = refs[-1]
    scores = [lax.dot_general(q, k[...], (((1,), (1,)), ((), ())), preferred_element_type=F32) for k in k_refs]
    m = jnp.max(scores[0], axis=-1, keepdims=True)
    for sc in scores[1:]:
        m = jnp.maximum(m, jnp.max(sc, axis=-1, keepdims=True))
    num = 0.0
    den = 0.0
    for sc, v in zip(scores, v_refs):
        p = jnp.exp(sc - m)
        den = den + jnp.sum(p, axis=-1, keepdims=True)
        num = num + jnp.dot(p.astype(BF16), v[...], preferred_element_type=F32)
    o_ref[...] = (num / den).astype(BF16)


def _attention(q, parts, nseq, t, row0, tq):
    nparts = len(parts)
    qb0 = row0 // tq
    nq = t // tq
    k_specs, v_specs, ks, vs = [], [], [], []
    for k_arr, v_arr, s_len, k_row0 in parts:
        kb0 = k_row0 // s_len
        k_specs.append(pl.BlockSpec((s_len, HQ), lambda s, h, i, kb0=kb0: (kb0 + s, h)))
        v_specs.append(pl.BlockSpec((s_len, DK), lambda s, h, i, kb0=kb0: (kb0 + s, h)))
        ks.append(k_arr)
        vs.append(v_arr)
    return pl.pallas_call(
        functools.partial(_attn_body, nparts=nparts),
        grid=(nseq, H, nq),
        in_specs=[pl.BlockSpec((tq, HQ), lambda s, h, i: (qb0 + s * nq + i, h))] + k_specs + v_specs,
        out_specs=pl.BlockSpec((tq, DK), lambda s, h, i: (s * nq + i, h)),
        out_shape=jax.ShapeDtypeStruct((nseq * t, H * DK), BF16),
        compiler_params=_cp("arbitrary", "arbitrary", "arbitrary"),
        name="attention",
    )(q, *ks, *vs)


def _out_body(x_ref, dn_ref, mla_ref, w_ref, m_ref, o_ref):
    w = w_ref[...]
    y = jnp.dot(dn_ref[...], w[:H * DK, :], preferred_element_type=F32)
    y = y + jnp.dot(mla_ref[...], w[H * DK:, :], preferred_element_type=F32)
    o_ref[...] = x_ref[...] + m_ref[2:3, :] * y


def _out_proj(x, dn, mla, w_out, mods, l, cond_of_row):
    n = x.shape[0]
    tm, tn = 512, 512
    return pl.pallas_call(
        _out_body,
        grid=(n // tm, D // tn),
        in_specs=[pl.BlockSpec((tm, tn), lambda i, j: (i, j)),
                  pl.BlockSpec((tm, H * DK), lambda i, j: (i, 0)),
                  pl.BlockSpec((tm, H * DK), lambda i, j: (i, 0)),
                  pl.BlockSpec((None, 2 * H * DK, tn), lambda i, j: (l, 0, j)),
                  pl.BlockSpec((None, None, N_MOD, tn), lambda i, j: (l, cond_of_row(i * tm), 0, j))],
        out_specs=pl.BlockSpec((tm, tn), lambda i, j: (i, j)),
        out_shape=jax.ShapeDtypeStruct((n, D), F32),
        compiler_params=_cp("arbitrary", "arbitrary"),
        name="out_proj",
    )(x, dn, mla, w_out, mods)


def _group_lane(x, k, lane):
    return jnp.where((lane & 3) + k < 4, pltpu.roll(x, LANE - k, axis=1), pltpu.roll(x, 4 - k, axis=1))


def _route_body(x_ref, m_ref, g_ref, wr_ref, br_ref, h_ref, comb_ref):
    x = x_ref[...]
    m = m_ref[...]
    h = x * lax.rsqrt(jnp.mean(x * x, axis=-1, keepdims=True) + EPS) * g_ref[...] * (1.0 + m[4:5, :]) + m[3:4, :]
    h_ref[...] = h.astype(BF16)
    tm = x.shape[0]
    lane = lax.broadcasted_iota(jnp.int32, (tm, LANE), 1)
    valid = lane < N_EXP
    scores = _sigmoid(_mm_f32(h, wr_ref[...]))
    sel = jnp.where(valid, scores + br_ref[...], NEG)
    rank = jnp.zeros((tm, LANE), F32)
    for k in (1, 2, 3):
        other = _group_lane(sel, k, lane)
        other_first = (lane & 3) + k >= 4
        beats = jnp.logical_or(other > sel, jnp.logical_and(other == sel, other_first))
        rank = rank + jnp.where(beats, 1.0, 0.0)
    top2 = rank < 2.0
    t = jnp.where(top2, sel, 0.0)
    gscore = t + _group_lane(t, 1, lane) + _group_lane(t, 2, lane) + _group_lane(t, 3, lane)
    lost = jnp.zeros((tm, LANE), F32)
    for k in (1, 2, 3):
        wrapped = lane + 4 * k >= N_EXP
        other = jnp.where(wrapped, pltpu.roll(gscore, N_EXP - 4 * k, axis=1), pltpu.roll(gscore, LANE - 4 * k, axis=1))
        loses = jnp.logical_or(other > gscore, jnp.logical_and(other == gscore, wrapped))
        lost = lost + jnp.where(loses, 1.0, 0.0)
    chosen = jnp.logical_and(jnp.logical_and(lost == 0.0, top2), valid)
    num = jnp.where(chosen, scores, 0.0)
    comb_ref[...] = num / jnp.sum(num, axis=-1, keepdims=True)


def _route(x, mods, g_ffn, w_router, b_router, l, cond_of_row):
    n = x.shape[0]
    tm = 512
    return pl.pallas_call(
        _route_body,
        grid=(n // tm,),
        in_specs=[pl.BlockSpec((tm, D), lambda i: (i, 0)),
                  pl.BlockSpec((None, None, N_MOD, D), lambda i: (l, cond_of_row(i * tm), 0, 0)),
                  pl.BlockSpec((None, 1, D), lambda i: (l, 0, 0)),
                  pl.BlockSpec((D, LANE), lambda i: (0, 0)),
                  pl.BlockSpec((1, LANE), lambda i: (0, 0))],
        out_specs=[pl.BlockSpec((tm, D), lambda i: (i, 0)), pl.BlockSpec((tm, LANE), lambda i: (i, 0))],
        out_shape=[jax.ShapeDtypeStruct((n, D), BF16), jax.ShapeDtypeStruct((n, LANE), F32)],
        compiler_params=_cp("arbitrary"),
        name="route",
    )(x, mods, g_ffn.reshape(DEPTH, 1, D), w_router, b_router)


def _moe_body(h_ref, comb_ref, wg_ref, wu_ref, wd_ref, x_ref, m_ref, o_ref, acc_s):
    e = pl.program_id(1)

    @pl.when(e == 0)
    def _():
        acc_s[...] = jnp.zeros_like(acc_s)

    h = h_ref[...]
    comb = comb_ref[...]
    lane = lax.broadcasted_iota(jnp.int32, comb.shape, 1)
    ce = jnp.sum(jnp.where(lane == e, comb, 0.0), axis=1, keepdims=True)
    hg = jnp.dot(h, wg_ref[...], preferred_element_type=F32)
    hu = jnp.dot(h, wu_ref[...], preferred_element_type=F32)
    act = hg * _sigmoid(hg) * hu * ce
    acc_s[...] += jnp.dot(act.astype(BF16), wd_ref[...], preferred_element_type=F32)

    @pl.when(e == N_EXP - 1)
    def _():
        o_ref[...] = x_ref[...] + m_ref[5:6, :] * acc_s[...]


def _moe(h, comb, w_gate, w_up, w_down, x, mods, l, cond_of_row):
    n = x.shape[0]
    tm = 512
    return pl.pallas_call(
        _moe_body,
        grid=(n // tm, N_EXP),
        in_specs=[pl.BlockSpec((tm, D), lambda i, e: (i, 0)),
                  pl.BlockSpec((tm, LANE), lambda i, e: (i, 0)),
                  pl.BlockSpec((None, None, D, D_FF), lambda i, e: (l, e, 0, 0)),
                  pl.BlockSpec((None, None, D, D_FF), lambda i, e: (l, e, 0, 0)),
                  pl.BlockSpec((None, None, D_FF, D), lambda i, e: (l, e, 0, 0)),
                  pl.BlockSpec((tm, D), lambda i, e: (i, 0)),
                  pl.BlockSpec((None, None, N_MOD, D), lambda i, e: (l, cond_of_row(i * tm), 0, 0))],
        out_specs=pl.BlockSpec((tm, D), lambda i, e: (i, 0)),
        out_shape=jax.ShapeDtypeStruct((n, D), F32),
        scratch_shapes=[pltpu.VMEM((tm, D), F32)],
        compiler_params=_cp("arbitrary", "arbitrary"),
        name="experts",
    )(h, comb, w_gate, w_up, w_down, x, mods)


def _rope_tables(n_pos, n_identity):
    t = jnp.arange(n_pos)
    quarter = ROPE // 4
    inv = ROPE_BASE ** (-jnp.arange(quarter, dtype=F32) / quarter)
    ang_r = (t // GRID_W).astype(F32)[:, None] * inv
    ang_c = (t % GRID_W).astype(F32)[:, None] * inv
    one = jnp.ones((n_pos, LANE - ROPE), F32)
    cos = jnp.concatenate([jnp.cos(ang_r), jnp.cos(ang_r), jnp.cos(ang_c), jnp.cos(ang_c), one], axis=1)
    sin = jnp.concatenate([-jnp.sin(ang_r), jnp.sin(ang_r), -jnp.sin(ang_c), jnp.sin(ang_c), 0.0 * one], axis=1)
    cos = jnp.concatenate([cos, jnp.ones((n_identity, LANE), F32)], axis=0)
    sin = jnp.concatenate([sin, jnp.zeros((n_identity, LANE), F32)], axis=0)
    return cos, sin


def _pad_heads(w):
    lead = w.shape[:-1]
    w = w.reshape(lead + (H, QK_DIM))
    w = jnp.pad(w, [(0, 0)] * len(lead) + [(0, 0), (0, HQ - QK_DIM)])
    return w.reshape(lead + (H * HQ,))


def kernel(x_prompt, x_sample, cache_ckv, cache_krope, state_delta, c, c_ctx, g_mix, w_mod, b_mod, w_in, conv_w,
           a_log, dt_bias, g_dn_out, g_qa, w_uq, g_kva, w_ukv, g_qh, g_kh, w_out, g_ffn, w_router, b_router,
           w_gate, w_up, w_down):
    nb, seq, _ = x_prompt.shape
    ndb, dseq, _ = x_sample.shape
    n_ctx = nb * seq
    past = cache_ckv.shape[2]

    def cond_of_row(r):
        return jnp.where(r < n_ctx, 0, 1 + (r - n_ctx) // dseq)

    def pos_block(r, tm):
        return jnp.where(r < n_ctx, dseq // tm, ((r - n_ctx) % dseq) // tm)

    conds = jnp.concatenate([c_ctx[None, :], c, jnp.zeros((8 - 1 - ndb, D), F32)], axis=0)
    o_z = QKVZ
    o_a, o_b = o_z, o_z + 2 * H
    o_cq = o_b + 2 * H
    o_ckv = o_cq + Q_LORA
    o_kr = o_ckv + KV_LORA
    zc = lambda k: jnp.zeros((DEPTH, D, k), F32)
    w_tail = jnp.concatenate([w_in[:, :, o_cq:o_ckv], w_in[:, :, o_ckv:o_kr], w_in[:, :, o_kr:o_kr + ROPE],
                              zc(LANE - ROPE), w_in[:, :, o_a:o_cq], zc(LANE - 4 * H)], axis=2).astype(BF16)
    pad_lane = lambda v: jnp.pad(v.reshape(DEPTH, 1, -1), ((0, 0), (0, 0), (0, LANE - 2 * H)))
    alog_p = pad_lane(a_log)
    dtb_p = pad_lane(dt_bias)
    wq_p = _pad_heads(w_uq).astype(BF16)
    wkv_p = w_ukv.astype(BF16)
    gq_p = jnp.pad(g_qh, ((0, 0), (0, HQ - QK_DIM))).reshape(DEPTH, 1, HQ)
    gk_p = jnp.pad(g_kh, ((0, 0), (0, HQ - QK_DIM))).reshape(DEPTH, 1, HQ)
    g_qa_p = g_qa.reshape(DEPTH, 1, Q_LORA)
    g_kva_p = g_kva.reshape(DEPTH, 1, KV_LORA)
    cache_kr_p = jnp.pad(cache_krope, ((0, 0), (0, 0), (0, 0), (0, LANE - ROPE)))
    w_out_b = w_out.astype(BF16)
    wr_p = jnp.pad(w_router, ((0, 0), (0, LANE - N_EXP)))
    br_p = jnp.pad(b_router, (0, LANE - N_EXP)).reshape(1, LANE)
    w_gate_b, w_up_b, w_down_b = w_gate.astype(BF16), w_up.astype(BF16), w_down.astype(BF16)
    cos_t, sin_t = _rope_tables(dseq, 512)
    s0_ctx = jnp.zeros((nb, 2, H, DK, DK), F32)

    mods = _modulation(conds, w_mod, b_mod)
    x = jnp.concatenate([x_prompt.reshape(n_ctx, D), x_sample.reshape(ndb * dseq, D)], axis=0)

    ckv_list, krope_list, state_list = [], [], []
    for l in range(DEPTH):
        proj = _in_proj(x, mods, g_mix, w_in, w_tail, l, cond_of_row)
        dn_ctx, s_ctx = _deltanet(proj, conv_w, alog_p, dtb_p, g_dn_out, s0_ctx, l, seq, 0, 8, 1)
        dn_lat, _ = _deltanet(proj, conv_w, alog_p, dtb_p, g_dn_out, state_delta[:, l], l, dseq, n_ctx, 2, 4)
        q, k, v, ckvn, krope = _mla_prep(proj, wq_p, wkv_p, g_qa_p, g_kva_p, gq_p, gk_p, cos_t, sin_t, l, pos_block)
        kc, vc = _cache_prep(cache_ckv, cache_kr_p, wkv_p, gk_p, l)
        mla_ctx = _attention(q, [(k, v, seq, 0)], nb, seq, 0, seq)
        mla_lat = _attention(q, [(kc, vc, past, 0), (k, v, dseq, n_ctx)], ndb, dseq, n_ctx, 512)
        dn = jnp.concatenate([dn_ctx, dn_lat], axis=0)
        mla = jnp.concatenate([mla_ctx, mla_lat], axis=0)
        x = _out_proj(x, dn, mla, w_out_b, mods, l, cond_of_row)
        h, comb = _route(x, mods, g_ffn, wr_p, br_p, l, cond_of_row)
        x = _moe(h, comb, w_gate_b, w_up_b, w_down_b, x, mods, l, cond_of_row)
        ckv_list.append(ckvn[:n_ctx].reshape(nb, seq, KV_LORA))
        krope_list.append(krope[:n_ctx].reshape(nb, seq, ROPE))
        state_list.append(s_ctx)

    y_prompt = x[:n_ctx].reshape(nb, seq, D)
    y_sample = x[n_ctx:].reshape(ndb, dseq, D)
    return (y_prompt, y_sample, jnp.stack(ckv_list, axis=1), jnp.stack(krope_list, axis=1),
            jnp.stack(state_list, axis=1))
```

```python
import functools

import jax
import jax.numpy as jnp
from jax import lax
from jax.experimental import pallas as pl
from jax.experimental.pallas import tpu as pltpu

F32 = jnp.float32
BF16 = jnp.bfloat16

D = 2048
DEPTH = 4
GRID_W = 64
H = 8
DK = 128
CONV_W = 5
CHUNK = 64
Q_LORA = 512
KV_LORA = 256
ROPE = 64
QK_DIM = DK + ROPE
ROPE_BASE = 10000.0
N_EXP = 16
D_FF = 512
N_MOD = 6
EPS = 1e-6
NEG = -1e30

LANE = 128
QKVZ = 4 * H * DK
TAIL = 1024
PROJ = QKVZ + TAIL
HQ = 2 * LANE

VMEM_LIMIT = 56 * 1024 * 1024


def _cp(*sem):
    return pltpu.CompilerParams(dimension_semantics=sem, vmem_limit_bytes=VMEM_LIMIT)


def _sigmoid(x):
    return 1.0 / (1.0 + jnp.exp(-x))


def _softplus(x):
    return jnp.maximum(x, 0.0) + jnp.log(1.0 + jnp.exp(-jnp.abs(x)))


def _mm(a, b):
    return jnp.dot(a.astype(BF16), b.astype(BF16), preferred_element_type=F32)


def _mm_f32(a, b):
    return jnp.dot(a, b, preferred_element_type=F32, precision=lax.Precision.HIGHEST)


def _mod_body(c_ref, w_ref, b_ref, o_ref):
    c = c_ref[...]
    o_ref[...] = _mm(c * _sigmoid(c), w_ref[...]) + b_ref[...]


def _modulation(conds, w_mod, b_mod):
    tn = 1024
    out = pl.pallas_call(
        _mod_body,
        grid=(DEPTH, N_MOD * D // tn),
        in_specs=[
            pl.BlockSpec((8, D), lambda l, j: (0, 0)),
            pl.BlockSpec((None, D, tn), lambda l, j: (l, 0, j)),
            pl.BlockSpec((None, 1, tn), lambda l, j: (l, 0, j)),
        ],
        out_specs=pl.BlockSpec((None, 8, tn), lambda l, j: (l, 0, j)),
        out_shape=jax.ShapeDtypeStruct((DEPTH, 8, N_MOD * D), F32),
        compiler_params=_cp("arbitrary", "arbitrary"),
        name="modulation",
    )(conds, w_mod, b_mod.reshape(DEPTH, 1, N_MOD * D))
    return out.reshape(DEPTH, 8, N_MOD, D)


def _in_body(x_ref, m_ref, g_ref, wm_ref, wt_ref, o_ref, h_scr, *, n_main):
    j = pl.program_id(1)

    @pl.when(j == 0)
    def _():
        x = x_ref[...]
        r = lax.rsqrt(jnp.mean(x * x, axis=-1, keepdims=True) + EPS)
        m = m_ref[...]
        h_scr[...] = (x * r * g_ref[...] * (1.0 + m[1:2, :]) + m[0:1, :]).astype(BF16)

    @pl.when(j < n_main)
    def _():
        o_ref[...] = jnp.dot(h_scr[...], wm_ref[...].astype(BF16), preferred_element_type=F32)

    @pl.when(j >= n_main)
    def _():
        o_ref[...] = jnp.dot(h_scr[...], wt_ref[...], preferred_element_type=F32)


def _in_proj(x, mods, g_mix, w_in, w_tail, l, cond_of_row):
    n = x.shape[0]
    tm, tn = 1024, 512
    n_main = QKVZ // tn
    n_tail = TAIL // tn
    return pl.pallas_call(
        functools.partial(_in_body, n_main=n_main),
        grid=(n // tm, n_main + n_tail),
        in_specs=[
            pl.BlockSpec((tm, D), lambda i, j: (i, 0)),
            pl.BlockSpec((None, None, N_MOD, D), lambda i, j: (l, cond_of_row(i * tm), 0, 0)),
            pl.BlockSpec((None, 1, D), lambda i, j: (l, 0, 0)),
            pl.BlockSpec((None, D, tn), lambda i, j: (l, 0, jnp.minimum(j, n_main - 1))),
            pl.BlockSpec((None, D, tn), lambda i, j: (l, 0, jnp.maximum(j - n_main, 0))),
        ],
        out_specs=pl.BlockSpec((tm, tn), lambda i, j: (i, j)),
        out_shape=jax.ShapeDtypeStruct((n, PROJ), F32),
        scratch_shapes=[pltpu.VMEM((tm, D), BF16)],
        compiler_params=_cp("arbitrary", "arbitrary"),
        name="in_proj",
    )(x, mods, g_mix.reshape(DEPTH, 1, D), w_in, w_tail)


def _bmm(a, b):
    return jnp.einsum('bij,bjk->bik', a.astype(BF16), b.astype(BF16), preferred_element_type=F32)


def _bmm_nt(a, b):
    return jnp.einsum('bid,bjd->bij', a.astype(BF16), b.astype(BF16), preferred_element_type=F32)


def _bmm_tn(a, b):
    return jnp.einsum('bci,bcj->bij', a.astype(BF16), b.astype(BF16), preferred_element_type=F32)


def _unit_triangular_inverse(lmat, ii, jj):
    b16 = (ii >> 4) == (jj >> 4)
    b32 = (ii >> 5) == (jj >> 5)
    eye = jnp.where(ii == jj, 1.0, 0.0)
    ld = jnp.where(b16, lmat, 0.0)
    l1 = jnp.where(b32, lmat - ld, 0.0)
    l2 = jnp.where(b32, 0.0, lmat)
    p = eye - ld
    a = _bmm(ld, ld)
    p = p + _bmm(p, a)
    a = _bmm(a, a)
    p = p + _bmm(p, a)
    a = _bmm(a, a)
    p = p + _bmm(p, a)
    t32 = p - _bmm(_bmm(p, l1), p)
    return t32 - _bmm(_bmm(t32, l2), t32)


def _dn_body(q_ref, k_ref, v_ref, z_ref, ab_ref, cwq_ref, cwk_ref, cwv_ref, alog_ref, dtb_ref, gout_ref, s0_ref,
             o_ref, sfin_ref, q_s, k_s, v_s, gc_s, beta_s, u_s, w_s, qd_s, kd_s, at_s, o_s, *, t, nh, unroll):
    n_chunks = t // CHUNK
    span = unroll * CHUNK
    nb = nh * unroll
    row = lax.broadcasted_iota(jnp.int32, (t, LANE), 0)
    lane = lax.broadcasted_iota(jnp.int32, (t, LANE), 1)
    head0 = pl.program_id(1) * nh

    def conv_act(x, cw):
        acc = x * cw[2:3, :]
        for j in (0, 1, 3, 4):
            d = j - CONV_W // 2
            shifted = pltpu.roll(x, (-d) % t, axis=0)
            ok = jnp.logical_and(row + d >= 0, row + d < t)
            acc = acc + jnp.where(ok, shifted, 0.0) * cw[j:j + 1, :]
        return acc * _sigmoid(acc)

    def l2n(x):
        return x * lax.rsqrt(jnp.sum(x * x, axis=-1, keepdims=True) + EPS)

    ab = ab_ref[...]
    g_all = -jnp.exp(alog_ref[...]) * _softplus(ab + dtb_ref[...])
    beta_all = _sigmoid(ab)
    pre = g_all
    suf = g_all
    r_in = row & (CHUNK - 1)
    s = 1
    while s < CHUNK:
        pre = pre + jnp.where(r_in >= s, pltpu.roll(pre, s, axis=0), 0.0)
        suf = suf + jnp.where(r_in < CHUNK - s, pltpu.roll(suf, t - s, axis=0), 0.0)
        s *= 2

    def column(x, idx):
        c = jnp.sum(jnp.where(lane == idx, x, 0.0), axis=1, keepdims=True)
        return jnp.broadcast_to(c, (t, LANE))

    for h in range(nh):
        sl = slice(h * LANE, (h + 1) * LANE)
        q_s[h] = l2n(conv_act(q_ref[:, sl], cwq_ref[:, sl])) * (DK ** -0.5)
        k_s[h] = l2n(conv_act(k_ref[:, sl], cwk_ref[:, sl]))
        v_s[h] = conv_act(v_ref[:, sl], cwv_ref[:, sl])
        gc_s[0, h] = column(pre, head0 + h)
        gc_s[1, h] = column(suf, H + head0 + h)
        beta_s[0, h] = column(beta_all, 2 * H + head0 + h)
        beta_s[1, h] = column(beta_all, 3 * H + head0 + h)
    o_s[...] = jnp.zeros_like(o_s)

    ii = lax.broadcasted_iota(jnp.int32, (1, CHUNK, CHUNK), 1)
    jj = lax.broadcasted_iota(jnp.int32, (1, CHUNK, CHUNK), 2)
    causal = ((ii >= jj, ii > jj), (ii <= jj, ii < jj))

    def chunk_end_gate(gc, d):
        return gc[:, CHUNK - 1:CHUNK, :] if d == 0 else gc[:, 0:1, :]

    def prepare_group(g, carry):
        rows = pl.ds(pl.multiple_of(g * span, span), span)
        per_chunk = lambda x: x.reshape(nb, CHUNK, x.shape[-1])
        per_head = lambda x: x.reshape(nh, span, x.shape[-1])
        q = per_chunk(q_s[:, rows, :])
        k = per_chunk(k_s[:, rows, :])
        v = per_chunk(v_s[:, rows, :])
        kk = _bmm_nt(k, k)
        qk = _bmm_nt(q, k)
        lmats, rhs = [], []
        for d in (0, 1):
            incl, strict = causal[d]
            gc = per_chunk(gc_s[d, :, rows, :])
            beta = per_chunk(beta_s[d, :, rows, :])
            gc_c = gc[:, :, :CHUNK]
            gc_row = jnp.sum(jnp.where(ii == jj, gc_c, 0.0), axis=1, keepdims=True)
            decay = jnp.exp(jnp.where(incl, gc_c - gc_row, NEG))
            e_gc = jnp.exp(gc)
            lmats.append(jnp.where(strict, kk * beta[:, :, :CHUNK] * decay, 0.0))
            rhs.append(jnp.concatenate([v * beta, k * beta * e_gc], axis=-1))
            qd_s[d, :, rows, :] = per_head(q * e_gc).astype(BF16)
            kd_s[d, :, rows, :] = per_head(k * jnp.exp(chunk_end_gate(gc, d) - gc)).astype(BF16)
            at_s[d, :, rows, :] = per_head(jnp.where(incl, qk * decay, 0.0)).astype(BF16)
        tmat = _unit_triangular_inverse(jnp.concatenate(lmats, axis=0), ii, jj)
        uw = _bmm(tmat, jnp.concatenate(rhs, axis=0))
        for d in (0, 1):
            part = uw[d * nb:(d + 1) * nb]
            u_s[d, :, rows, :] = per_head(part[:, :, :LANE])
            w_s[d, :, rows, :] = per_head(part[:, :, LANE:]).astype(BF16)
        return carry

    lax.fori_loop(0, n_chunks // unroll, prepare_group, 0)

    def scan_step(i, states):
        rows = [pl.ds(pl.multiple_of(c * CHUNK, CHUNK), CHUNK) for c in (i, n_chunks - 1 - i)]
        r = [_bmm(jnp.concatenate([w_s[d, :, rows[d], :], qd_s[d, :, rows[d], :]], axis=1), states[d])
             for d in (0, 1)]
        v_new = [(u_s[d, :, rows[d], :] - r[d][:, :CHUNK, :]).astype(BF16) for d in (0, 1)]
        o_add = [r[d][:, CHUNK:, :] + _bmm(at_s[d, :, rows[d], :], v_new[d]) for d in (0, 1)]
        s_add = [_bmm_tn(kd_s[d, :, rows[d], :], v_new[d]) for d in (0, 1)]
        out = []
        for d in (0, 1):
            o_s[:, rows[d], :] += o_add[d]
            out.append(states[d] * jnp.exp(chunk_end_gate(gc_s[d, :, rows[d], :], d)) + s_add[d])
        return tuple(out)

    sf, sb = lax.fori_loop(0, n_chunks, scan_step, (s0_ref[0], s0_ref[1]))
    sfin_ref[0] = sf
    sfin_ref[1] = sb

    for h in range(nh):
        sl = slice(h * LANE, (h + 1) * LANE)
        o = o_s[h]
        y = o * lax.rsqrt(jnp.mean(o * o, axis=-1, keepdims=True) + EPS) * gout_ref[...]
        z = z_ref[:, sl]
        o_ref[:, sl] = (y * z * _sigmoid(z)).astype(BF16)


def _deltanet(proj, conv_w, alog, dtb, g_out, s0, l, t, row0, nh, unroll):
    nseq = s0.shape[0]
    rb0 = row0 // t
    w = nh * LANE
    ng = H // nh
    tok = lambda part: pl.BlockSpec((t, w), lambda s, g: (rb0 + s, part * ng + g))
    cw = lambda part: pl.BlockSpec((None, CONV_W, w), lambda s, g: (l, 0, part * ng + g))
    small = pl.BlockSpec((None, 1, LANE), lambda s, g: (l, 0, 0))
    st = pl.BlockSpec((None, 2, nh, DK, DK), lambda s, g: (s, 0, g, 0, 0))
    wide = lambda dt: pltpu.VMEM((2, nh, t, LANE), dt)
    return pl.pallas_call(
        functools.partial(_dn_body, t=t, nh=nh, unroll=unroll),
        grid=(nseq, ng),
        in_specs=[tok(0), tok(1), tok(2), tok(3),
                  pl.BlockSpec((t, LANE), lambda s, g: (rb0 + s, PROJ // LANE - 1)),
                  cw(0), cw(1), cw(2), small, small, small, st],
        out_specs=[pl.BlockSpec((t, w), lambda s, g: (s, g)), st],
        out_shape=[jax.ShapeDtypeStruct((nseq * t, H * DK), BF16),
                   jax.ShapeDtypeStruct((nseq, 2, H, DK, DK), F32)],
        scratch_shapes=[pltpu.VMEM((nh, t, LANE), F32), pltpu.VMEM((nh, t, LANE), F32), pltpu.VMEM((nh, t, LANE), F32),
                        wide(F32), wide(F32), wide(F32), wide(BF16), wide(BF16), wide(BF16),
                        pltpu.VMEM((2, nh, t, CHUNK), BF16), pltpu.VMEM((nh, t, LANE), F32)],
        compiler_params=_cp("arbitrary", "arbitrary"),
        name="deltanet",
    )(proj, proj, proj, proj, proj, conv_w, conv_w, conv_w, alog, dtb, g_out.reshape(DEPTH, 1, DK), s0)


def _rope_swap(x, lane):
    return jnp.where((lane & 31) < 16, pltpu.roll(x, LANE - 16, axis=1), pltpu.roll(x, 16, axis=1))


def _head_keys_values(kv, kr, gk, cos, sin, lane, k_ref, v_ref):
    kr_ss = jnp.sum(kr * kr, axis=-1, keepdims=True)
    for h in range(H):
        kn = kv[:, h * HQ:h * HQ + DK]
        r = lax.rsqrt((jnp.sum(kn * kn, axis=-1, keepdims=True) + kr_ss) * (1.0 / QK_DIM) + EPS)
        k_ref[:, h * HQ:h * HQ + DK] = (kn * r * gk[:, :DK]).astype(BF16)
        rr = kr * r * gk[:, DK:]
        if cos is not None:
            rr = rr * cos + _rope_swap(rr, lane) * sin
        k_ref[:, h * HQ + DK:(h + 1) * HQ] = rr.astype(BF16)
        v_ref[:, h * DK:(h + 1) * DK] = kv[:, h * HQ + DK:(h + 1) * HQ].astype(BF16)


def _mla_prep_body(cq_ref, ckv_ref, kr_ref, wq_ref, wkv_ref, gqa_ref, gkva_ref, gq_ref, gk_ref, cos_ref, sin_ref,
                   q_ref, k_ref, v_ref, ckvn_ref, krope_ref):
    tm = cq_ref.shape[0]
    lane = lax.broadcasted_iota(jnp.int32, (tm, LANE), 1)
    cos = cos_ref[...]
    sin = sin_ref[...]
    cq = cq_ref[...]
    cqn = cq * lax.rsqrt(jnp.mean(cq * cq, axis=-1, keepdims=True) + EPS) * gqa_ref[...]
    q = jnp.dot(cqn.astype(BF16), wq_ref[...], preferred_element_type=F32)
    gq = gq_ref[...]
    for h in range(H):
        qn = q[:, h * HQ:h * HQ + DK]
        qr = q[:, h * HQ + DK:(h + 1) * HQ]
        ss = jnp.sum(qn * qn, axis=-1, keepdims=True) + jnp.sum(qr * qr, axis=-1, keepdims=True)
        r = lax.rsqrt(ss * (1.0 / QK_DIM) + EPS) * (QK_DIM ** -0.5)
        q_ref[:, h * HQ:h * HQ + DK] = (qn * r * gq[:, :DK]).astype(BF16)
        qr = qr * r * gq[:, DK:]
        q_ref[:, h * HQ + DK:(h + 1) * HQ] = (qr * cos + _rope_swap(qr, lane) * sin).astype(BF16)
    ckv = ckv_ref[...]
    ckvn = ckv * lax.rsqrt(jnp.mean(ckv * ckv, axis=-1, keepdims=True) + EPS) * gkva_ref[...]
    ckvn_ref[...] = ckvn
    kr = kr_ref[...]
    krope_ref[...] = kr[:, :ROPE]
    kv = jnp.dot(ckvn.astype(BF16), wkv_ref[...], preferred_element_type=F32)
    _head_keys_values(kv, kr, gk_ref[...], cos, sin, lane, k_ref, v_ref)


def _mla_prep(proj, wq, wkv, g_qa, g_kva, gq, gk, cos_t, sin_t, l, pos_block):
    n = proj.shape[0]
    tm = 512
    lw = lambda shape: pl.BlockSpec((None,) + shape, lambda i: (l,) + (0,) * len(shape))
    table = pl.BlockSpec((tm, LANE), lambda i: (pos_block(i * tm, tm), 0))
    return pl.pallas_call(
        _mla_prep_body,
        grid=(n // tm,),
        in_specs=[pl.BlockSpec((tm, Q_LORA), lambda i: (i, QKVZ // Q_LORA)),
                  pl.BlockSpec((tm, KV_LORA), lambda i: (i, (QKVZ + Q_LORA) // KV_LORA)),
                  pl.BlockSpec((tm, LANE), lambda i: (i, (QKVZ + Q_LORA + KV_LORA) // LANE)),
                  lw((Q_LORA, H * HQ)), lw((KV_LORA, H * HQ)), lw((1, Q_LORA)), lw((1, KV_LORA)),
                  lw((1, HQ)), lw((1, HQ)), table, table],
        out_specs=[pl.BlockSpec((tm, H * HQ), lambda i: (i, 0)), pl.BlockSpec((tm, H * HQ), lambda i: (i, 0)),
                   pl.BlockSpec((tm, H * DK), lambda i: (i, 0)), pl.BlockSpec((tm, KV_LORA), lambda i: (i, 0)),
                   pl.BlockSpec((tm, ROPE), lambda i: (i, 0))],
        out_shape=[jax.ShapeDtypeStruct((n, H * HQ), BF16), jax.ShapeDtypeStruct((n, H * HQ), BF16),
                   jax.ShapeDtypeStruct((n, H * DK), BF16), jax.ShapeDtypeStruct((n, KV_LORA), F32),
                   jax.ShapeDtypeStruct((n, ROPE), F32)],
        compiler_params=_cp("arbitrary"),
        name="mla_prep",
    )(proj, proj, proj, wq, wkv, g_qa, g_kva, gq, gk, cos_t, sin_t)


def _cache_prep_body(ckv_ref, kr_ref, wkv_ref, gk_ref, k_ref, v_ref):
    kv = jnp.dot(ckv_ref[...].astype(BF16), wkv_ref[...], preferred_element_type=F32)
    _head_keys_values(kv, kr_ref[...], gk_ref[...], None, None, None, k_ref, v_ref)


def _cache_prep(cache_ckv, cache_kr, wkv, gk, l):
    nb, _, past, _ = cache_ckv.shape
    return pl.pallas_call(
        _cache_prep_body,
        grid=(nb,),
        in_specs=[pl.BlockSpec((None, None, past, KV_LORA), lambda b: (b, l, 0, 0)),
                  pl.BlockSpec((None, None, past, LANE), lambda b: (b, l, 0, 0)),
                  pl.BlockSpec((None, KV_LORA, H * HQ), lambda b: (l, 0, 0)),
                  pl.BlockSpec((None, 1, HQ), lambda b: (l, 0, 0))],
        out_specs=[pl.BlockSpec((past, H * HQ), lambda b: (b, 0)), pl.BlockSpec((past, H * DK), lambda b: (b, 0))],
        out_shape=[jax.ShapeDtypeStruct((nb * past, H * HQ), BF16), jax.ShapeDtypeStruct((nb * past, H * DK), BF16)],
        compiler_params=_cp("arbitrary"),
        name="cache_prep",
    )(cache_ckv, cache_kr, wkv, gk)


def _attn_body(*refs, nparts, hb):
    q_ref = refs[0]
    k_refs = refs[1:1 + nparts]
    v_refs = refs[1 + nparts:1 + 2 * nparts]
    o_ref = refs[-1]
    for h in range(hb):
        q = q_ref[:, h * HQ:(h + 1) * HQ]
        scores = [lax.dot_general(q, k[:, h * HQ:(h + 1) * HQ], (((1,), (1,)), ((), ())), preferred_element_type=F32)
                  for k in k_refs]
        m = jnp.max(scores[0], axis=-1, keepdims=True)
        for sc in scores[1:]:
            m = jnp.maximum(m, jnp.max(sc, axis=-1, keepdims=True))
        num = 0.0
        den = 0.0
        for sc, v in zip(scores, v_refs):
            p = jnp.exp(sc - m)
            den = den + jnp.sum(p, axis=-1, keepdims=True)
            num = num + jnp.dot(p.astype(BF16), v[:, h * DK:(h + 1) * DK], preferred_element_type=F32)
        o_ref[:, h * DK:(h + 1) * DK] = (num / den).astype(BF16)


def _attention(q, parts, nseq, t, row0, tq, hb):
    nparts = len(parts)
    qb0 = row0 // tq
    nq = t // tq
    k_specs, v_specs, ks, vs = [], [], [], []
    for k_arr, v_arr, s_len, k_row0 in parts:
        kb0 = k_row0 // s_len
        k_specs.append(pl.BlockSpec((s_len, hb * HQ), lambda s, h, i, kb0=kb0: (kb0 + s, h)))
        v_specs.append(pl.BlockSpec((s_len, hb * DK), lambda s, h, i, kb0=kb0: (kb0 + s, h)))
        ks.append(k_arr)
        vs.append(v_arr)
    return pl.pallas_call(
        functools.partial(_attn_body, nparts=nparts, hb=hb),
        grid=(nseq, H // hb, nq),
        in_specs=[pl.BlockSpec((tq, hb * HQ), lambda s, h, i: (qb0 + s * nq + i, h))] + k_specs + v_specs,
        out_specs=pl.BlockSpec((tq, hb * DK), lambda s, h, i: (s * nq + i, h)),
        out_shape=jax.ShapeDtypeStruct((nseq * t, H * DK), BF16),
        compiler_params=_cp("arbitrary", "arbitrary", "arbitrary"),
        name="attention",
    )(q, *ks, *vs)


def _out_body(x_ref, dnc_ref, mlac_ref, dnl_ref, mlal_ref, w_ref, m_ref, o_ref, *, ctx_tiles):
    w = w_ref[...]

    def emit(dn_ref, mla_ref):
        y = jnp.dot(dn_ref[...], w[:H * DK, :], preferred_element_type=F32)
        y = y + jnp.dot(mla_ref[...], w[H * DK:, :], preferred_element_type=F32)
        o_ref[...] = x_ref[...] + m_ref[2:3, :] * y

    @pl.when(pl.program_id(0) < ctx_tiles)
    def _():
        emit(dnc_ref, mlac_ref)

    @pl.when(pl.program_id(0) >= ctx_tiles)
    def _():
        emit(dnl_ref, mlal_ref)


def _out_proj(x, dn_ctx, mla_ctx, dn_lat, mla_lat, w_out, mods, l, cond_of_row):
    n = x.shape[0]
    tm, tn = 512, 512
    ctx_tiles = dn_ctx.shape[0] // tm
    ctx = pl.BlockSpec((tm, H * DK), lambda i, j: (jnp.minimum(i, ctx_tiles - 1), 0))
    lat = pl.BlockSpec((tm, H * DK), lambda i, j: (jnp.maximum(i - ctx_tiles, 0), 0))
    return pl.pallas_call(
        functools.partial(_out_body, ctx_tiles=ctx_tiles),
        grid=(n // tm, D // tn),
        in_specs=[pl.BlockSpec((tm, tn), lambda i, j: (i, j)), ctx, ctx, lat, lat,
                  pl.BlockSpec((None, 2 * H * DK, tn), lambda i, j: (l, 0, j)),
                  pl.BlockSpec((None, None, N_MOD, tn), lambda i, j: (l, cond_of_row(i * tm), 0, j))],
        out_specs=pl.BlockSpec((tm, tn), lambda i, j: (i, j)),
        out_shape=jax.ShapeDtypeStruct((n, D), F32),
        compiler_params=_cp("arbitrary", "arbitrary"),
        name="out_proj",
    )(x, dn_ctx, mla_ctx, dn_lat, mla_lat, w_out, mods)


def _group_lane(x, k, lane):
    return jnp.where((lane & 3) + k < 4, pltpu.roll(x, LANE - k, axis=1), pltpu.roll(x, 4 - k, axis=1))


def _route_body(x_ref, m_ref, g_ref, wr_ref, br_ref, h_ref, rt_ref, cnt_ref, run_s):
    @pl.when(pl.program_id(0) == 0)
    def _():
        run_s[...] = jnp.zeros_like(run_s)

    x = x_ref[...]
    m = m_ref[...]
    h = x * lax.rsqrt(jnp.mean(x * x, axis=-1, keepdims=True) + EPS) * g_ref[...] * (1.0 + m[4:5, :]) + m[3:4, :]
    h_ref[...] = h
    tm = x.shape[0]
    lane = lax.broadcasted_iota(jnp.int32, (tm, LANE), 1)
    valid = lane < N_EXP
    scores = _sigmoid(_mm_f32(h, wr_ref[...]))
    sel = jnp.where(valid, scores + br_ref[...], NEG)
    rank = jnp.zeros((tm, LANE), F32)
    for k in (1, 2, 3):
        other = _group_lane(sel, k, lane)
        other_first = (lane & 3) + k >= 4
        beats = jnp.logical_or(other > sel, jnp.logical_and(other == sel, other_first))
        rank = rank + jnp.where(beats, 1.0, 0.0)
    top2 = rank < 2.0
    t = jnp.where(top2, sel, 0.0)
    gscore = t + _group_lane(t, 1, lane) + _group_lane(t, 2, lane) + _group_lane(t, 3, lane)
    lost = jnp.zeros((tm, LANE), F32)
    for k in (1, 2, 3):
        wrapped = lane + 4 * k >= N_EXP
        other = jnp.where(wrapped, pltpu.roll(gscore, N_EXP - 4 * k, axis=1), pltpu.roll(gscore, LANE - 4 * k, axis=1))
        loses = jnp.logical_or(other > gscore, jnp.logical_and(other == gscore, wrapped))
        lost = lost + jnp.where(loses, 1.0, 0.0)
    chosen = jnp.logical_and(jnp.logical_and(lost == 0.0, top2), valid)
    num = jnp.where(chosen, scores, 0.0)
    gate = num / jnp.sum(num, axis=-1, keepdims=True)
    member = jnp.where(chosen, 1.0, 0.0)
    r_i = lax.broadcasted_iota(jnp.int32, (tm, tm), 0)
    c_i = lax.broadcasted_iota(jnp.int32, (tm, tm), 1)
    earlier = jnp.where(r_i > c_i, 1.0, 0.0)
    pos = _mm(earlier, member) + run_s[0:1, :]
    run_s[...] = run_s[...] + jnp.sum(member, axis=0, keepdims=True)
    cnt_ref[...] = run_s[...]
    lane_f = lane.astype(F32)
    rec = jnp.zeros((tm, LANE), F32)
    for k in (0, 1):
        pick = jnp.logical_and(chosen, rank == float(k))
        for field, val in ((0, lane_f), (2, pos), (4, gate)):
            col = jnp.sum(jnp.where(pick, val, 0.0), axis=-1, keepdims=True)
            rec = jnp.where(lane == field + k, col, rec)
    rt_ref[...] = rec


def _route(x, mods, g_ffn, w_router, b_router, l, cond_of_row):
    n = x.shape[0]
    tm = 512
    return pl.pallas_call(
        _route_body,
        grid=(n // tm,),
        in_specs=[pl.BlockSpec((tm, D), lambda i: (i, 0)),
                  pl.BlockSpec((None, None, N_MOD, D), lambda i: (l, cond_of_row(i * tm), 0, 0)),
                  pl.BlockSpec((None, 1, D), lambda i: (l, 0, 0)),
                  pl.BlockSpec((D, LANE), lambda i: (0, 0)),
                  pl.BlockSpec((1, LANE), lambda i: (0, 0))],
        out_specs=[pl.BlockSpec((tm, D), lambda i: (i, 0)), pl.BlockSpec((tm, LANE), lambda i: (i, 0)),
                   pl.BlockSpec((8, LANE), lambda i: (0, 0))],
        out_shape=[jax.ShapeDtypeStruct((n, D), F32), jax.ShapeDtypeStruct((n, LANE), F32),
                   jax.ShapeDtypeStruct((8, LANE), F32)],
        scratch_shapes=[pltpu.VMEM((8, LANE), F32)],
        compiler_params=_cp("arbitrary"),
        name="route",
    )(x, mods, g_ffn.reshape(DEPTH, 1, D), w_router, b_router)


EXPERT_TILE = 512
TOP_K = 2
DMA_WINDOW = 512


def _slot_plan(rt, cnt, n):
    n_tiles = (TOP_K * n) // EXPERT_TILE + N_EXP
    counts = cnt[0, :N_EXP].astype(jnp.int32)
    padded = ((counts + EXPERT_TILE - 1) // EXPERT_TILE) * EXPERT_TILE
    ends = jnp.cumsum(padded)
    starts = ends - padded
    expert = rt[:, 0:TOP_K].astype(jnp.int32)
    slots = (starts[expert] + rt[:, 2:2 + TOP_K].astype(jnp.int32)).T.reshape(-1)
    n_active = (ends[-1] // EXPERT_TILE).reshape(1)
    tile_expert = jnp.minimum(jnp.searchsorted(ends, jnp.arange(n_tiles) * EXPERT_TILE, side="right"), N_EXP - 1)
    tile = jnp.arange(n_tiles)
    is_last = jnp.any(jnp.logical_and(padded[None, :] > 0, tile[:, None] == ends[None, :] // EXPERT_TILE - 1), axis=1)
    zero_tile = jnp.logical_or(is_last, tile >= n_active[0])
    return slots.astype(jnp.int32), tile_expert.astype(jnp.int32), n_active.astype(jnp.int32), \
        zero_tile.astype(jnp.int32), n_tiles


def _dispatch_body(slots_ref, zero_ref, h_hbm, xs_hbm, zero_s, sem, *, n):
    zero_s[...] = jnp.zeros_like(zero_s)
    n_tiles = xs_hbm.shape[0] // EXPERT_TILE

    def zero_copy(t):
        dst = xs_hbm.at[pl.ds(pl.multiple_of(t * EXPERT_TILE, EXPERT_TILE), EXPERT_TILE)]
        return pltpu.make_async_copy(zero_s, dst, sem)

    def zero_start(t, c):
        @pl.when(zero_ref[t] == 1)
        def _():
            zero_copy(t).start()
        return c

    def zero_wait(t, c):
        @pl.when(zero_ref[t] == 1)
        def _():
            zero_copy(t).wait()
        return c

    lax.fori_loop(0, n_tiles, zero_start, 0)
    lax.fori_loop(0, n_tiles, zero_wait, 0)

    def row_copy(i):
        tok = jnp.where(i < n, i, i - n)
        return pltpu.make_async_copy(h_hbm.at[pl.ds(tok, 1)], xs_hbm.at[pl.ds(slots_ref[i], 1)], sem)

    def window(w, c):
        def issue(i, c):
            row_copy(w * DMA_WINDOW + i).start()
            return c

        def drain(i, c):
            row_copy(w * DMA_WINDOW + i).wait()
            return c

        lax.fori_loop(0, DMA_WINDOW, issue, 0)
        lax.fori_loop(0, DMA_WINDOW, drain, 0)
        return c

    lax.fori_loop(0, TOP_K * n // DMA_WINDOW, window, 0)


def _dispatch(h, slots, zero_tile, n_tiles):
    n = h.shape[0]
    return pl.pallas_call(
        functools.partial(_dispatch_body, n=n),
        grid_spec=pltpu.PrefetchScalarGridSpec(
            num_scalar_prefetch=2, grid=(1,),
            in_specs=[pl.BlockSpec(memory_space=pl.ANY)],
            out_specs=pl.BlockSpec(memory_space=pl.ANY),
            scratch_shapes=[pltpu.VMEM((EXPERT_TILE, D), F32), pltpu.SemaphoreType.DMA(())]),
        out_shape=jax.ShapeDtypeStruct((n_tiles * EXPERT_TILE, D), F32),
        compiler_params=_cp("arbitrary"),
        name="dispatch",
    )(slots, zero_tile, h)


def _experts_body(te_ref, na_ref, x_ref, wg_ref, wu_ref, wd_ref, y_ref):
    j = pl.program_id(0)

    @pl.when(j < na_ref[0])
    def _():
        x = x_ref[...].astype(BF16)
        hg = jnp.dot(x, wg_ref[...].astype(BF16), preferred_element_type=F32)
        hu = jnp.dot(x, wu_ref[...].astype(BF16), preferred_element_type=F32)
        act = (hg * _sigmoid(hg) * hu).astype(BF16)
        y_ref[...] = jnp.dot(act, wd_ref[...].astype(BF16), preferred_element_type=F32)

    @pl.when(j >= na_ref[0])
    def _():
        y_ref[...] = jnp.zeros_like(y_ref)


def _experts(xs, tile_expert, n_active, w_gate, w_up, w_down, l, n_tiles):
    wspec = lambda a, b: pl.BlockSpec((None, None, a, b), lambda j, te, na: (l, te[j], 0, 0))
    return pl.pallas_call(
        _experts_body,
        grid_spec=pltpu.PrefetchScalarGridSpec(
            num_scalar_prefetch=2, grid=(n_tiles,),
            in_specs=[pl.BlockSpec((EXPERT_TILE, D), lambda j, te, na: (jnp.minimum(j, na[0] - 1), 0)),
                      wspec(D, D_FF), wspec(D, D_FF), wspec(D_FF, D)],
            out_specs=pl.BlockSpec((EXPERT_TILE, D), lambda j, te, na: (j, 0))),
        out_shape=jax.ShapeDtypeStruct((n_tiles * EXPERT_TILE, D), F32),
        compiler_params=_cp("arbitrary"),
        name="experts",
    )(tile_expert, n_active, xs, w_gate, w_up, w_down)


def _combine_body(slots_ref, y_hbm, x_ref, rt_ref, m_ref, o_ref, y_s, sem, *, n, tm):
    base = pl.program_id(0) * tm

    def row_copy(r, k):
        return pltpu.make_async_copy(y_hbm.at[pl.ds(slots_ref[k * n + base + r], 1)], y_s.at[k, pl.ds(r, 1)], sem)

    def issue(r, c):
        for k in range(TOP_K):
            row_copy(r, k).start()
        return c

    def drain(r, c):
        for k in range(TOP_K):
            row_copy(r, k).wait()
        return c

    lax.fori_loop(0, tm, issue, 0)
    lax.fori_loop(0, tm, drain, 0)
    rt = rt_ref[...]
    o_ref[...] = x_ref[...] + m_ref[5:6, :] * (rt[:, 4:5] * y_s[0] + rt[:, 5:6] * y_s[1])


def _combine(y, slots, x, rt, mods, l, cond_of_row):
    n = x.shape[0]
    tm = 256
    return pl.pallas_call(
        functools.partial(_combine_body, n=n, tm=tm),
        grid_spec=pltpu.PrefetchScalarGridSpec(
            num_scalar_prefetch=1, grid=(n // tm,),
            in_specs=[pl.BlockSpec(memory_space=pl.ANY),
                      pl.BlockSpec((tm, D), lambda i, s: (i, 0)),
                      pl.BlockSpec((tm, LANE), lambda i, s: (i, 0)),
                      pl.BlockSpec((None, None, N_MOD, D), lambda i, s: (l, cond_of_row(i * tm), 0, 0))],
            out_specs=pl.BlockSpec((tm, D), lambda i, s: (i, 0)),
            scratch_shapes=[pltpu.VMEM((TOP_K, tm, D), F32), pltpu.SemaphoreType.DMA(())]),
        out_shape=jax.ShapeDtypeStruct((n, D), F32),
        compiler_params=_cp("arbitrary"),
        name="combine",
    )(slots, y, x, rt, mods)


def _rope_tables(n_pos, n_identity):
    t = jnp.arange(n_pos)
    quarter = ROPE // 4
    inv = ROPE_BASE ** (-jnp.arange(quarter, dtype=F32) / quarter)
    ang_r = (t // GRID_W).astype(F32)[:, None] * inv
    ang_c = (t % GRID_W).astype(F32)[:, None] * inv
    one = jnp.ones((n_pos, LANE - ROPE), F32)
    cos = jnp.concatenate([jnp.cos(ang_r), jnp.cos(ang_r), jnp.cos(ang_c), jnp.cos(ang_c), one], axis=1)
    sin = jnp.concatenate([-jnp.sin(ang_r), jnp.sin(ang_r), -jnp.sin(ang_c), jnp.sin(ang_c), 0.0 * one], axis=1)
    cos = jnp.concatenate([cos, jnp.ones((n_identity, LANE), F32)], axis=0)
    sin = jnp.concatenate([sin, jnp.zeros((n_identity, LANE), F32)], axis=0)
    return cos, sin


def _pad_heads(w):
    lead = w.shape[:-1]
    w = w.reshape(lead + (H, QK_DIM))
    w = jnp.pad(w, [(0, 0)] * len(lead) + [(0, 0), (0, HQ - QK_DIM)])
    return w.reshape(lead + (H * HQ,))


def kernel(x_prompt, x_sample, cache_ckv, cache_krope, state_delta, c, c_ctx, g_mix, w_mod, b_mod, w_in, conv_w,
           a_log, dt_bias, g_dn_out, g_qa, w_uq, g_kva, w_ukv, g_qh, g_kh, w_out, g_ffn, w_router, b_router,
           w_gate, w_up, w_down):
    nb, seq, _ = x_prompt.shape
    ndb, dseq, _ = x_sample.shape
    n_ctx = nb * seq
    past = cache_ckv.shape[2]

    def cond_of_row(r):
        return jnp.where(r < n_ctx, 0, 1 + (r - n_ctx) // dseq)

    def pos_block(r, tm):
        return jnp.where(r < n_ctx, dseq // tm, ((r - n_ctx) % dseq) // tm)

    conds = jnp.concatenate([c_ctx[None, :], c, jnp.zeros((8 - 1 - ndb, D), F32)], axis=0)
    o_z = QKVZ
    o_a, o_b = o_z, o_z + 2 * H
    o_cq = o_b + 2 * H
    o_ckv = o_cq + Q_LORA
    o_kr = o_ckv + KV_LORA
    zc = lambda k: jnp.zeros((DEPTH, D, k), F32)
    w_tail = jnp.concatenate([w_in[:, :, o_cq:o_ckv], w_in[:, :, o_ckv:o_kr], w_in[:, :, o_kr:o_kr + ROPE],
                              zc(LANE - ROPE), w_in[:, :, o_a:o_cq], zc(LANE - 4 * H)], axis=2).astype(BF16)
    pad_lane = lambda v: jnp.pad(v.reshape(DEPTH, 1, -1), ((0, 0), (0, 0), (0, LANE - 2 * H)))
    alog_p = pad_lane(a_log)
    dtb_p = pad_lane(dt_bias)
    wq_p = _pad_heads(w_uq).astype(BF16)
    wkv_p = w_ukv.astype(BF16)
    gq_p = jnp.pad(g_qh, ((0, 0), (0, HQ - QK_DIM))).reshape(DEPTH, 1, HQ)
    gk_p = jnp.pad(g_kh, ((0, 0), (0, HQ - QK_DIM))).reshape(DEPTH, 1, HQ)
    g_qa_p = g_qa.reshape(DEPTH, 1, Q_LORA)
    g_kva_p = g_kva.reshape(DEPTH, 1, KV_LORA)
    cache_kr_p = jnp.pad(cache_krope, ((0, 0), (0, 0), (0, 0), (0, LANE - ROPE)))
    w_out_b = w_out.astype(BF16)
    wr_p = jnp.pad(w_router, ((0, 0), (0, LANE - N_EXP)))
    br_p = jnp.pad(b_router, (0, LANE - N_EXP)).reshape(1, LANE)
    cos_t, sin_t = _rope_tables(dseq, 512)
    s0_ctx = jnp.zeros((nb, 2, H, DK, DK), F32)

    mods = _modulation(conds, w_mod, b_mod)
    x = jnp.concatenate([x_prompt.reshape(n_ctx, D), x_sample.reshape(ndb * dseq, D)], axis=0)

    ckv_list, krope_list, state_list = [], [], []
    for l in range(DEPTH):
        proj = _in_proj(x, mods, g_mix, w_in, w_tail, l, cond_of_row)
        dn_ctx, s_ctx = _deltanet(proj, conv_w, alog_p, dtb_p, g_dn_out, s0_ctx, l, seq, 0, 8, 1)
        dn_lat, _ = _deltanet(proj, conv_w, alog_p, dtb_p, g_dn_out, state_delta[:, l], l, dseq, n_ctx, 2, 4)
        q, k, v, ckvn, krope = _mla_prep(proj, wq_p, wkv_p, g_qa_p, g_kva_p, gq_p, gk_p, cos_t, sin_t, l, pos_block)
        kc, vc = _cache_prep(cache_ckv, cache_kr_p, wkv_p, gk_p, l)
        mla_ctx = _attention(q, [(k, v, seq, 0)], nb, seq, 0, seq, H)
        mla_lat = _attention(q, [(kc, vc, past, 0), (k, v, dseq, n_ctx)], ndb, dseq, n_ctx, 512, 2)
        x = _out_proj(x, dn_ctx, mla_ctx, dn_lat, mla_lat, w_out_b, mods, l, cond_of_row)
        h, rt, cnt = _route(x, mods, g_ffn, wr_p, br_p, l, cond_of_row)
        slots, tile_expert, n_active, zero_tile, n_tiles = _slot_plan(rt, cnt, x.shape[0])
        xs = _dispatch(h, slots, zero_tile, n_tiles)
        y = _experts(xs, tile_expert, n_active, w_gate, w_up, w_down, l, n_tiles)
        x = _combine(y, slots, x, rt, mods, l, cond_of_row)
        ckv_list.append(ckvn[:n_ctx].reshape(nb, seq, KV_LORA))
        krope_list.append(krope[:n_ctx].reshape(nb, seq, ROPE))
        state_list.append(s_ctx)

    y_prompt = x[:n_ctx].reshape(nb, seq, D)
    y_sample = x[n_ctx:].reshape(ndb, dseq, D)
    return (y_prompt, y_sample, jnp.stack(ckv_list, axis=1), jnp.stack(krope_list, axis=1),
            jnp.stack(state_list, axis=1))
```

```python
import functools

import jax
import jax.numpy as jnp
from jax import lax
from jax.experimental import pallas as pl
from jax.experimental.pallas import tpu as pltpu

F32 = jnp.float32
BF16 = jnp.bfloat16

D = 2048
DEPTH = 4
GRID_W = 64
H = 8
DK = 128
CONV_W = 5
CHUNK = 64
Q_LORA = 512
KV_LORA = 256
ROPE = 64
QK_DIM = DK + ROPE
ROPE_BASE = 10000.0
N_EXP = 16
D_FF = 512
N_MOD = 6
EPS = 1e-6
NEG = -1e30

LANE = 128
QKVZ = 4 * H * DK
TAIL = 1024
PROJ = QKVZ + TAIL
HQ = 2 * LANE

VMEM_LIMIT = 56 * 1024 * 1024


def _cp(*sem):
    return pltpu.CompilerParams(dimension_semantics=sem, vmem_limit_bytes=VMEM_LIMIT)


def _sigmoid(x):
    return 1.0 / (1.0 + jnp.exp(-x))


def _softplus(x):
    return jnp.maximum(x, 0.0) + jnp.log(1.0 + jnp.exp(-jnp.abs(x)))


def _mm(a, b):
    return jnp.dot(a.astype(BF16), b.astype(BF16), preferred_element_type=F32)


def _mm_f32(a, b):
    return jnp.dot(a, b, preferred_element_type=F32, precision=lax.Precision.HIGHEST)


def _mod_body(c_ref, w_ref, b_ref, o_ref):
    c = c_ref[...]
    o_ref[...] = _mm(c * _sigmoid(c), w_ref[...]) + b_ref[...]


def _modulation(conds, w_mod, b_mod):
    tn = 1024
    out = pl.pallas_call(
        _mod_body,
        grid=(DEPTH, N_MOD * D // tn),
        in_specs=[
            pl.BlockSpec((8, D), lambda l, j: (0, 0)),
            pl.BlockSpec((None, D, tn), lambda l, j: (l, 0, j)),
            pl.BlockSpec((None, 1, tn), lambda l, j: (l, 0, j)),
        ],
        out_specs=pl.BlockSpec((None, 8, tn), lambda l, j: (l, 0, j)),
        out_shape=jax.ShapeDtypeStruct((DEPTH, 8, N_MOD * D), F32),
        compiler_params=_cp("arbitrary", "arbitrary"),
        name="modulation",
    )(conds, w_mod, b_mod.reshape(DEPTH, 1, N_MOD * D))
    return out.reshape(DEPTH, 8, N_MOD, D)


def _in_body(x_ref, m_ref, g_ref, wm_ref, wt_ref, o_ref, h_scr, *, n_main):
    j = pl.program_id(1)

    @pl.when(j == 0)
    def _():
        x = x_ref[...]
        r = lax.rsqrt(jnp.mean(x * x, axis=-1, keepdims=True) + EPS)
        m = m_ref[...]
        h_scr[...] = (x * r * g_ref[...] * (1.0 + m[1:2, :]) + m[0:1, :]).astype(BF16)

    @pl.when(j < n_main)
    def _():
        o_ref[...] = jnp.dot(h_scr[...], wm_ref[...].astype(BF16), preferred_element_type=F32)

    @pl.when(j >= n_main)
    def _():
        o_ref[...] = jnp.dot(h_scr[...], wt_ref[...], preferred_element_type=F32)


def _in_proj(x, mods, g_mix, w_in, w_tail, l, cond_of_row):
    n = x.shape[0]
    tm, tn = 1024, 512
    n_main = QKVZ // tn
    n_tail = TAIL // tn
    return pl.pallas_call(
        functools.partial(_in_body, n_main=n_main),
        grid=(n // tm, n_main + n_tail),
        in_specs=[
            pl.BlockSpec((tm, D), lambda i, j: (i, 0)),
            pl.BlockSpec((None, None, N_MOD, D), lambda i, j: (l, cond_of_row(i * tm), 0, 0)),
            pl.BlockSpec((None, 1, D), lambda i, j: (l, 0, 0)),
            pl.BlockSpec((None, D, tn), lambda i, j: (l, 0, jnp.minimum(j, n_main - 1))),
            pl.BlockSpec((None, D, tn), lambda i, j: (l, 0, jnp.maximum(j - n_main, 0))),
        ],
        out_specs=pl.BlockSpec((tm, tn), lambda i, j: (i, j)),
        out_shape=jax.ShapeDtypeStruct((n, PROJ), F32),
        scratch_shapes=[pltpu.VMEM((tm, D), BF16)],
        compiler_params=_cp("arbitrary", "arbitrary"),
        name="in_proj",
    )(x, mods, g_mix.reshape(DEPTH, 1, D), w_in, w_tail)


def _bmm(a, b):
    return jnp.einsum('bij,bjk->bik', a.astype(BF16), b.astype(BF16), preferred_element_type=F32)


def _bmm_nt(a, b):
    return jnp.einsum('bid,bjd->bij', a.astype(BF16), b.astype(BF16), preferred_element_type=F32)


def _bmm_tn(a, b):
    return jnp.einsum('bci,bcj->bij', a.astype(BF16), b.astype(BF16), preferred_element_type=F32)


def _unit_triangular_inverse(lmat, ii, jj):
    b16 = (ii >> 4) == (jj >> 4)
    b32 = (ii >> 5) == (jj >> 5)
    eye = jnp.where(ii == jj, 1.0, 0.0)
    ld = jnp.where(b16, lmat, 0.0)
    l1 = jnp.where(b32, lmat - ld, 0.0)
    l2 = jnp.where(b32, 0.0, lmat)
    p = eye - ld
    a = _bmm(ld, ld)
    p = p + _bmm(p, a)
    a = _bmm(a, a)
    p = p + _bmm(p, a)
    a = _bmm(a, a)
    p = p + _bmm(p, a)
    t32 = p - _bmm(_bmm(p, l1), p)
    return t32 - _bmm(_bmm(t32, l2), t32)


def _dn_body(q_ref, k_ref, v_ref, z_ref, ab_ref, cwq_ref, cwk_ref, cwv_ref, alog_ref, dtb_ref, gout_ref, s0_ref,
             o_ref, sfin_ref, q_s, k_s, v_s, gc_s, beta_s, u_s, w_s, qd_s, kd_s, at_s, o_s, gate_s, *, t, nh, unroll):
    n_chunks = t // CHUNK
    span = unroll * CHUNK
    nb = nh * unroll
    row = lax.broadcasted_iota(jnp.int32, (t, LANE), 0)
    lane = lax.broadcasted_iota(jnp.int32, (t, LANE), 1)
    head0 = pl.program_id(1) * nh

    def conv_act(x, cw):
        acc = x * cw[2:3, :]
        for j in (0, 1, 3, 4):
            d = j - CONV_W // 2
            shifted = pltpu.roll(x, (-d) % t, axis=0)
            ok = jnp.logical_and(row + d >= 0, row + d < t)
            acc = acc + jnp.where(ok, shifted, 0.0) * cw[j:j + 1, :]
        return acc * _sigmoid(acc)

    def l2n(x):
        return x * lax.rsqrt(jnp.sum(x * x, axis=-1, keepdims=True) + EPS)

    @pl.when(pl.program_id(1) == 0)
    def _():
        ab = ab_ref[...]
        g_all = -jnp.exp(alog_ref[...]) * _softplus(ab + dtb_ref[...])
        pre = g_all
        suf = g_all
        r_in = row & (CHUNK - 1)
        s = 1
        while s < CHUNK:
            pre = pre + jnp.where(r_in >= s, pltpu.roll(pre, s, axis=0), 0.0)
            suf = suf + jnp.where(r_in < CHUNK - s, pltpu.roll(suf, t - s, axis=0), 0.0)
            s *= 2
        gate_s[0] = pre
        gate_s[1] = suf
        gate_s[2] = _sigmoid(ab)

    pre = gate_s[0]
    suf = gate_s[1]
    beta_all = gate_s[2]

    def column(x, idx):
        c = jnp.sum(jnp.where(lane == idx, x, 0.0), axis=1, keepdims=True)
        return jnp.broadcast_to(c, (t, LANE))

    for h in range(nh):
        sl = slice(h * LANE, (h + 1) * LANE)
        q_s[h] = l2n(conv_act(q_ref[:, sl], cwq_ref[:, sl])) * (DK ** -0.5)
        k_s[h] = l2n(conv_act(k_ref[:, sl], cwk_ref[:, sl]))
        v_s[h] = conv_act(v_ref[:, sl], cwv_ref[:, sl])
        gc_s[0, h] = column(pre, head0 + h)
        gc_s[1, h] = column(suf, H + head0 + h)
        beta_s[0, h] = column(beta_all, 2 * H + head0 + h)
        beta_s[1, h] = column(beta_all, 3 * H + head0 + h)
    o_s[...] = jnp.zeros_like(o_s)

    ii = lax.broadcasted_iota(jnp.int32, (1, CHUNK, CHUNK), 1)
    jj = lax.broadcasted_iota(jnp.int32, (1, CHUNK, CHUNK), 2)
    causal = ((ii >= jj, ii > jj), (ii <= jj, ii < jj))

    def chunk_end_gate(gc, d):
        return gc[:, CHUNK - 1:CHUNK, :] if d == 0 else gc[:, 0:1, :]

    def prepare_group(g, carry):
        rows = pl.ds(pl.multiple_of(g * span, span), span)
        per_chunk = lambda x: x.reshape(nb, CHUNK, x.shape[-1])
        per_head = lambda x: x.reshape(nh, span, x.shape[-1])
        q = per_chunk(q_s[:, rows, :])
        k = per_chunk(k_s[:, rows, :])
        v = per_chunk(v_s[:, rows, :])
        kk = _bmm_nt(k, k)
        qk = _bmm_nt(q, k)
        lmats, rhs = [], []
        for d in (0, 1):
            incl, strict = causal[d]
            gc = per_chunk(gc_s[d, :, rows, :])
            beta = per_chunk(beta_s[d, :, rows, :])
            gc_c = gc[:, :, :CHUNK]
            gc_row = jnp.sum(jnp.where(ii == jj, gc_c, 0.0), axis=1, keepdims=True)
            decay = jnp.exp(jnp.where(incl, gc_c - gc_row, NEG))
            e_gc = jnp.exp(gc)
            lmats.append(jnp.where(strict, kk * beta[:, :, :CHUNK] * decay, 0.0))
            rhs.append(jnp.concatenate([v * beta, k * beta * e_gc], axis=-1))
            qd_s[d, :, rows, :] = per_head(q * e_gc).astype(BF16)
            kd_s[d, :, rows, :] = per_head(k * jnp.exp(chunk_end_gate(gc, d) - gc)).astype(BF16)
            at_s[d, :, rows, :] = per_head(jnp.where(incl, qk * decay, 0.0)).astype(BF16)
        tmat = _unit_triangular_inverse(jnp.concatenate(lmats, axis=0), ii, jj)
        uw = _bmm(tmat, jnp.concatenate(rhs, axis=0))
        for d in (0, 1):
            part = uw[d * nb:(d + 1) * nb]
            u_s[d, :, rows, :] = per_head(part[:, :, :LANE])
            w_s[d, :, rows, :] = per_head(part[:, :, LANE:]).astype(BF16)
        return carry

    lax.fori_loop(0, n_chunks // unroll, prepare_group, 0)

    def scan_step(i, states):
        rows = [pl.ds(pl.multiple_of(c * CHUNK, CHUNK), CHUNK) for c in (i, n_chunks - 1 - i)]
        r = [_bmm(jnp.concatenate([w_s[d, :, rows[d], :], qd_s[d, :, rows[d], :]], axis=1), states[d])
             for d in (0, 1)]
        v_new = [(u_s[d, :, rows[d], :] - r[d][:, :CHUNK, :]).astype(BF16) for d in (0, 1)]
        o_add = [r[d][:, CHUNK:, :] + _bmm(at_s[d, :, rows[d], :], v_new[d]) for d in (0, 1)]
        s_add = [_bmm_tn(kd_s[d, :, rows[d], :], v_new[d]) for d in (0, 1)]
        out = []
        for d in (0, 1):
            o_s[:, rows[d], :] += o_add[d]
            out.append(states[d] * jnp.exp(chunk_end_gate(gc_s[d, :, rows[d], :], d)) + s_add[d])
        return tuple(out)

    sf, sb = lax.fori_loop(0, n_chunks, scan_step, (s0_ref[0], s0_ref[1]))
    sfin_ref[0] = sf
    sfin_ref[1] = sb

    for h in range(nh):
        sl = slice(h * LANE, (h + 1) * LANE)
        o = o_s[h]
        y = o * lax.rsqrt(jnp.mean(o * o, axis=-1, keepdims=True) + EPS) * gout_ref[...]
        z = z_ref[:, sl]
        o_ref[:, sl] = (y * z * _sigmoid(z)).astype(BF16)


def _deltanet(proj, conv_w, alog, dtb, g_out, s0, l, t, row0, nh, unroll):
    nseq = s0.shape[0]
    rb0 = row0 // t
    w = nh * LANE
    ng = H // nh
    tok = lambda part: pl.BlockSpec((t, w), lambda s, g: (rb0 + s, part * ng + g))
    cw = lambda part: pl.BlockSpec((None, CONV_W, w), lambda s, g: (l, 0, part * ng + g))
    small = pl.BlockSpec((None, 1, LANE), lambda s, g: (l, 0, 0))
    st = pl.BlockSpec((None, 2, nh, DK, DK), lambda s, g: (s, 0, g, 0, 0))
    wide = lambda dt: pltpu.VMEM((2, nh, t, LANE), dt)
    return pl.pallas_call(
        functools.partial(_dn_body, t=t, nh=nh, unroll=unroll),
        grid=(nseq, ng),
        in_specs=[tok(0), tok(1), tok(2), tok(3),
                  pl.BlockSpec((t, LANE), lambda s, g: (rb0 + s, PROJ // LANE - 1)),
                  cw(0), cw(1), cw(2), small, small, small, st],
        out_specs=[pl.BlockSpec((t, w), lambda s, g: (s, g)), st],
        out_shape=[jax.ShapeDtypeStruct((nseq * t, H * DK), BF16),
                   jax.ShapeDtypeStruct((nseq, 2, H, DK, DK), F32)],
        scratch_shapes=[pltpu.VMEM((nh, t, LANE), F32), pltpu.VMEM((nh, t, LANE), F32), pltpu.VMEM((nh, t, LANE), F32),
                        wide(F32), wide(F32), wide(F32), wide(BF16), wide(BF16), wide(BF16),
                        pltpu.VMEM((2, nh, t, CHUNK), BF16), pltpu.VMEM((nh, t, LANE), F32),
                        pltpu.VMEM((3, t, LANE), F32)],
        compiler_params=_cp("arbitrary", "arbitrary"),
        name="deltanet",
    )(proj, proj, proj, proj, proj, conv_w, conv_w, conv_w, alog, dtb, g_out.reshape(DEPTH, 1, DK), s0)


def _rope_swap(x, lane):
    return jnp.where((lane & 31) < 16, pltpu.roll(x, LANE - 16, axis=1), pltpu.roll(x, 16, axis=1))


def _head_keys_values(kv, kr, gk, cos, sin, lane, k_ref, v_ref):
    kr_ss = jnp.sum(kr * kr, axis=-1, keepdims=True)
    for h in range(H):
        kn = kv[:, h * HQ:h * HQ + DK]
        r = lax.rsqrt((jnp.sum(kn * kn, axis=-1, keepdims=True) + kr_ss) * (1.0 / QK_DIM) + EPS)
        k_ref[:, h * HQ:h * HQ + DK] = (kn * r * gk[:, :DK]).astype(BF16)
        rr = kr * r * gk[:, DK:]
        if cos is not None:
            rr = rr * cos + _rope_swap(rr, lane) * sin
        k_ref[:, h * HQ + DK:(h + 1) * HQ] = rr.astype(BF16)
        v_ref[:, h * DK:(h + 1) * DK] = kv[:, h * HQ + DK:(h + 1) * HQ].astype(BF16)


def _mla_prep_body(cq_ref, ckv_ref, kr_ref, wq_ref, wkv_ref, gqa_ref, gkva_ref, gq_ref, gk_ref, cos_ref, sin_ref,
                   q_ref, k_ref, v_ref, ckvn_ref, krope_ref):
    tm = cq_ref.shape[0]
    lane = lax.broadcasted_iota(jnp.int32, (tm, LANE), 1)
    cos = cos_ref[...]
    sin = sin_ref[...]
    cq = cq_ref[...]
    cqn = cq * lax.rsqrt(jnp.mean(cq * cq, axis=-1, keepdims=True) + EPS) * gqa_ref[...]
    q = jnp.dot(cqn.astype(BF16), wq_ref[...], preferred_element_type=F32)
    gq = gq_ref[...]
    for h in range(H):
        qn = q[:, h * HQ:h * HQ + DK]
        qr = q[:, h * HQ + DK:(h + 1) * HQ]
        ss = jnp.sum(qn * qn, axis=-1, keepdims=True) + jnp.sum(qr * qr, axis=-1, keepdims=True)
        r = lax.rsqrt(ss * (1.0 / QK_DIM) + EPS) * (QK_DIM ** -0.5)
        q_ref[:, h * HQ:h * HQ + DK] = (qn * r * gq[:, :DK]).astype(BF16)
        qr = qr * r * gq[:, DK:]
        q_ref[:, h * HQ + DK:(h + 1) * HQ] = (qr * cos + _rope_swap(qr, lane) * sin).astype(BF16)
    ckv = ckv_ref[...]
    ckvn = ckv * lax.rsqrt(jnp.mean(ckv * ckv, axis=-1, keepdims=True) + EPS) * gkva_ref[...]
    ckvn_ref[...] = ckvn
    kr = kr_ref[...]
    krope_ref[...] = kr[:, :ROPE]
    kv = jnp.dot(ckvn.astype(BF16), wkv_ref[...], preferred_element_type=F32)
    _head_keys_values(kv, kr, gk_ref[...], cos, sin, lane, k_ref, v_ref)


def _mla_prep(proj, wq, wkv, g_qa, g_kva, gq, gk, cos_t, sin_t, l, pos_block):
    n = proj.shape[0]
    tm = 512
    lw = lambda shape: pl.BlockSpec((None,) + shape, lambda i: (l,) + (0,) * len(shape))
    table = pl.BlockSpec((tm, LANE), lambda i: (pos_block(i * tm, tm), 0))
    return pl.pallas_call(
        _mla_prep_body,
        grid=(n // tm,),
        in_specs=[pl.BlockSpec((tm, Q_LORA), lambda i: (i, QKVZ // Q_LORA)),
                  pl.BlockSpec((tm, KV_LORA), lambda i: (i, (QKVZ + Q_LORA) // KV_LORA)),
                  pl.BlockSpec((tm, LANE), lambda i: (i, (QKVZ + Q_LORA + KV_LORA) // LANE)),
                  lw((Q_LORA, H * HQ)), lw((KV_LORA, H * HQ)), lw((1, Q_LORA)), lw((1, KV_LORA)),
                  lw((1, HQ)), lw((1, HQ)), table, table],
        out_specs=[pl.BlockSpec((tm, H * HQ), lambda i: (i, 0)), pl.BlockSpec((tm, H * HQ), lambda i: (i, 0)),
                   pl.BlockSpec((tm, H * DK), lambda i: (i, 0)), pl.BlockSpec((tm, KV_LORA), lambda i: (i, 0)),
                   pl.BlockSpec((tm, ROPE), lambda i: (i, 0))],
        out_shape=[jax.ShapeDtypeStruct((n, H * HQ), BF16), jax.ShapeDtypeStruct((n, H * HQ), BF16),
                   jax.ShapeDtypeStruct((n, H * DK), BF16), jax.ShapeDtypeStruct((n, KV_LORA), F32),
                   jax.ShapeDtypeStruct((n, ROPE), F32)],
        compiler_params=_cp("arbitrary"),
        name="mla_prep",
    )(proj, proj, proj, wq, wkv, g_qa, g_kva, gq, gk, cos_t, sin_t)


def _cache_prep_body(ckv_ref, kr_ref, wkv_ref, gk_ref, k_ref, v_ref):
    kv = jnp.dot(ckv_ref[...].astype(BF16), wkv_ref[...], preferred_element_type=F32)
    _head_keys_values(kv, kr_ref[...], gk_ref[...], None, None, None, k_ref, v_ref)


def _cache_prep(cache_ckv, cache_kr, wkv, gk, l):
    nb, _, past, _ = cache_ckv.shape
    return pl.pallas_call(
        _cache_prep_body,
        grid=(nb,),
        in_specs=[pl.BlockSpec((None, None, past, KV_LORA), lambda b: (b, l, 0, 0)),
                  pl.BlockSpec((None, None, past, LANE), lambda b: (b, l, 0, 0)),
                  pl.BlockSpec((None, KV_LORA, H * HQ), lambda b: (l, 0, 0)),
                  pl.BlockSpec((None, 1, HQ), lambda b: (l, 0, 0))],
        out_specs=[pl.BlockSpec((past, H * HQ), lambda b: (b, 0)), pl.BlockSpec((past, H * DK), lambda b: (b, 0))],
        out_shape=[jax.ShapeDtypeStruct((nb * past, H * HQ), BF16), jax.ShapeDtypeStruct((nb * past, H * DK), BF16)],
        compiler_params=_cp("arbitrary"),
        name="cache_prep",
    )(cache_ckv, cache_kr, wkv, gk)


def _attn_body(*refs, nparts, hb):
    q_ref = refs[0]
    k_refs = refs[1:1 + nparts]
    v_refs = refs[1 + nparts:1 + 2 * nparts]
    o_ref = refs[-1]
    for h in range(hb):
        q = q_ref[:, h * HQ:(h + 1) * HQ]
        scores = [lax.dot_general(q, k[:, h * HQ:(h + 1) * HQ], (((1,), (1,)), ((), ())), preferred_element_type=F32)
                  for k in k_refs]
        m = jnp.max(scores[0], axis=-1, keepdims=True)
        for sc in scores[1:]:
            m = jnp.maximum(m, jnp.max(sc, axis=-1, keepdims=True))
        num = 0.0
        den = 0.0
        for sc, v in zip(scores, v_refs):
            p = jnp.exp(sc - m)
            den = den + jnp.sum(p, axis=-1, keepdims=True)
            num = num + jnp.dot(p.astype(BF16), v[:, h * DK:(h + 1) * DK], preferred_element_type=F32)
        o_ref[:, h * DK:(h + 1) * DK] = (num / den).astype(BF16)


def _attention(q, parts, nseq, t, row0, tq, hb):
    nparts = len(parts)
    qb0 = row0 // tq
    nq = t // tq
    k_specs, v_specs, ks, vs = [], [], [], []
    for k_arr, v_arr, s_len, k_row0 in parts:
        kb0 = k_row0 // s_len
        k_specs.append(pl.BlockSpec((s_len, hb * HQ), lambda s, h, i, kb0=kb0: (kb0 + s, h)))
        v_specs.append(pl.BlockSpec((s_len, hb * DK), lambda s, h, i, kb0=kb0: (kb0 + s, h)))
        ks.append(k_arr)
        vs.append(v_arr)
    return pl.pallas_call(
        functools.partial(_attn_body, nparts=nparts, hb=hb),
        grid=(nseq, H // hb, nq),
        in_specs=[pl.BlockSpec((tq, hb * HQ), lambda s, h, i: (qb0 + s * nq + i, h))] + k_specs + v_specs,
        out_specs=pl.BlockSpec((tq, hb * DK), lambda s, h, i: (s * nq + i, h)),
        out_shape=jax.ShapeDtypeStruct((nseq * t, H * DK), BF16),
        compiler_params=_cp("arbitrary", "arbitrary", "arbitrary"),
        name="attention",
    )(q, *ks, *vs)


def _out_body(x_ref, dnc_ref, mlac_ref, dnl_ref, mlal_ref, w_ref, m_ref, o_ref, *, ctx_tiles):
    w = w_ref[...]

    def emit(dn_ref, mla_ref):
        y = jnp.dot(dn_ref[...], w[:H * DK, :], preferred_element_type=F32)
        y = y + jnp.dot(mla_ref[...], w[H * DK:, :], preferred_element_type=F32)
        o_ref[...] = x_ref[...] + m_ref[2:3, :] * y

    @pl.when(pl.program_id(0) < ctx_tiles)
    def _():
        emit(dnc_ref, mlac_ref)

    @pl.when(pl.program_id(0) >= ctx_tiles)
    def _():
        emit(dnl_ref, mlal_ref)


def _out_proj(x, dn_ctx, mla_ctx, dn_lat, mla_lat, w_out, mods, l, cond_of_row):
    n = x.shape[0]
    tm, tn = 512, 512
    ctx_tiles = dn_ctx.shape[0] // tm
    ctx = pl.BlockSpec((tm, H * DK), lambda i, j: (jnp.minimum(i, ctx_tiles - 1), 0))
    lat = pl.BlockSpec((tm, H * DK), lambda i, j: (jnp.maximum(i - ctx_tiles, 0), 0))
    return pl.pallas_call(
        functools.partial(_out_body, ctx_tiles=ctx_tiles),
        grid=(n // tm, D // tn),
        in_specs=[pl.BlockSpec((tm, tn), lambda i, j: (i, j)), ctx, ctx, lat, lat,
                  pl.BlockSpec((None, 2 * H * DK, tn), lambda i, j: (l, 0, j)),
                  pl.BlockSpec((None, None, N_MOD, tn), lambda i, j: (l, cond_of_row(i * tm), 0, j))],
        out_specs=pl.BlockSpec((tm, tn), lambda i, j: (i, j)),
        out_shape=jax.ShapeDtypeStruct((n, D), F32),
        compiler_params=_cp("arbitrary", "arbitrary"),
        name="out_proj",
    )(x, dn_ctx, mla_ctx, dn_lat, mla_lat, w_out, mods)


def _group_lane(x, k, lane):
    return jnp.where((lane & 3) + k < 4, pltpu.roll(x, LANE - k, axis=1), pltpu.roll(x, 4 - k, axis=1))


def _route_body(x_ref, m_ref, g_ref, wr_ref, br_ref, h_ref, rt_ref, cnt_ref, run_s):
    @pl.when(pl.program_id(0) == 0)
    def _():
        run_s[...] = jnp.zeros_like(run_s)

    x = x_ref[...]
    m = m_ref[...]
    h = x * lax.rsqrt(jnp.mean(x * x, axis=-1, keepdims=True) + EPS) * g_ref[...] * (1.0 + m[4:5, :]) + m[3:4, :]
    h_ref[...] = h
    tm = x.shape[0]
    lane = lax.broadcasted_iota(jnp.int32, (tm, LANE), 1)
    valid = lane < N_EXP
    scores = _sigmoid(_mm_f32(h, wr_ref[...]))
    sel = jnp.where(valid, scores + br_ref[...], NEG)
    rank = jnp.zeros((tm, LANE), F32)
    for k in (1, 2, 3):
        other = _group_lane(sel, k, lane)
        other_first = (lane & 3) + k >= 4
        beats = jnp.logical_or(other > sel, jnp.logical_and(other == sel, other_first))
        rank = rank + jnp.where(beats, 1.0, 0.0)
    top2 = rank < 2.0
    t = jnp.where(top2, sel, 0.0)
    gscore = t + _group_lane(t, 1, lane) + _group_lane(t, 2, lane) + _group_lane(t, 3, lane)
    lost = jnp.zeros((tm, LANE), F32)
    for k in (1, 2, 3):
        wrapped = lane + 4 * k >= N_EXP
        other = jnp.where(wrapped, pltpu.roll(gscore, N_EXP - 4 * k, axis=1), pltpu.roll(gscore, LANE - 4 * k, axis=1))
        loses = jnp.logical_or(other > gscore, jnp.logical_and(other == gscore, wrapped))
        lost = lost + jnp.where(loses, 1.0, 0.0)
    chosen = jnp.logical_and(jnp.logical_and(lost == 0.0, top2), valid)
    num = jnp.where(chosen, scores, 0.0)
    gate = num / jnp.sum(num, axis=-1, keepdims=True)
    member = jnp.where(chosen, 1.0, 0.0)
    r_i = lax.broadcasted_iota(jnp.int32, (tm, tm), 0)
    c_i = lax.broadcasted_iota(jnp.int32, (tm, tm), 1)
    earlier = jnp.where(r_i > c_i, 1.0, 0.0)
    pos = _mm(earlier, member) + run_s[0:1, :]
    run_s[...] = run_s[...] + jnp.sum(member, axis=0, keepdims=True)
    cnt_ref[...] = run_s[...]
    lane_f = lane.astype(F32)
    rec = jnp.zeros((tm, LANE), F32)
    for k in (0, 1):
        pick = jnp.logical_and(chosen, rank == float(k))
        for field, val in ((0, lane_f), (2, pos), (4, gate)):
            col = jnp.sum(jnp.where(pick, val, 0.0), axis=-1, keepdims=True)
            rec = jnp.where(lane == field + k, col, rec)
    rt_ref[...] = rec


def _route(x, mods, g_ffn, w_router, b_router, l, cond_of_row):
    n = x.shape[0]
    tm = 512
    return pl.pallas_call(
        _route_body,
        grid=(n // tm,),
        in_specs=[pl.BlockSpec((tm, D), lambda i: (i, 0)),
                  pl.BlockSpec((None, None, N_MOD, D), lambda i: (l, cond_of_row(i * tm), 0, 0)),
                  pl.BlockSpec((None, 1, D), lambda i: (l, 0, 0)),
                  pl.BlockSpec((D, LANE), lambda i: (0, 0)),
                  pl.BlockSpec((1, LANE), lambda i: (0, 0))],
        out_specs=[pl.BlockSpec((tm, D), lambda i: (i, 0)), pl.BlockSpec((tm, LANE), lambda i: (i, 0)),
                   pl.BlockSpec((8, LANE), lambda i: (0, 0))],
        out_shape=[jax.ShapeDtypeStruct((n, D), F32), jax.ShapeDtypeStruct((n, LANE), F32),
                   jax.ShapeDtypeStruct((8, LANE), F32)],
        scratch_shapes=[pltpu.VMEM((8, LANE), F32)],
        compiler_params=_cp("arbitrary"),
        name="route",
    )(x, mods, g_ffn.reshape(DEPTH, 1, D), w_router, b_router)


EXPERT_TILE = 512
TOP_K = 2


def _slot_plan(rt, cnt, n):
    n_tiles = (TOP_K * n) // EXPERT_TILE + N_EXP
    counts = cnt[0, :N_EXP].astype(jnp.int32)
    padded = ((counts + EXPERT_TILE - 1) // EXPERT_TILE) * EXPERT_TILE
    ends = jnp.cumsum(padded)
    starts = ends - padded
    expert = rt[:, 0:TOP_K].astype(jnp.int32)
    slots = (starts[expert] + rt[:, 2:2 + TOP_K].astype(jnp.int32)).T.reshape(-1)
    n_active = (ends[-1] // EXPERT_TILE).reshape(1)
    tile_expert = jnp.minimum(jnp.searchsorted(ends, jnp.arange(n_tiles) * EXPERT_TILE, side="right"), N_EXP - 1)
    token = jnp.tile(jnp.arange(n, dtype=jnp.int32), TOP_K)
    source = jnp.zeros((n_tiles * EXPERT_TILE,), jnp.int32).at[slots].set(token)
    return slots.astype(jnp.int32), source, tile_expert.astype(jnp.int32), n_active.astype(jnp.int32), n_tiles


def _experts_body(te_ref, na_ref, src_ref, h_hbm, wg_ref, wu_ref, wd_ref, y_ref, x_s, sem):
    j = pl.program_id(0)
    n_active = na_ref[0]

    def row_copy(tile, r):
        src = h_hbm.at[pl.ds(src_ref[tile * EXPERT_TILE + r], 1)]
        return pltpu.make_async_copy(src, x_s.at[tile % 2, pl.ds(r, 1)], sem.at[tile % 2])

    def gather_start(tile):
        def issue(r, c):
            row_copy(tile, r).start()
            return c
        lax.fori_loop(0, EXPERT_TILE, issue, 0)

    def gather_wait(tile):
        def drain(r, c):
            row_copy(tile, r).wait()
            return c
        lax.fori_loop(0, EXPERT_TILE, drain, 0)

    @pl.when(j == 0)
    def _():
        gather_start(j)

    @pl.when(j + 1 < n_active)
    def _():
        gather_start(j + 1)

    @pl.when(j < n_active)
    def _():
        gather_wait(j)
        x = x_s[j % 2].astype(BF16)
        hg = jnp.dot(x, wg_ref[...].astype(BF16), preferred_element_type=F32)
        hu = jnp.dot(x, wu_ref[...].astype(BF16), preferred_element_type=F32)
        act = (hg * _sigmoid(hg) * hu).astype(BF16)
        y_ref[...] = jnp.dot(act, wd_ref[...].astype(BF16), preferred_element_type=F32)

    @pl.when(j >= n_active)
    def _():
        y_ref[...] = jnp.zeros_like(y_ref)


def _experts(h, source, tile_expert, n_active, w_gate, w_up, w_down, l, n_tiles):
    wspec = lambda a, b: pl.BlockSpec((None, None, a, b), lambda j, te, na, src: (l, te[j], 0, 0))
    return pl.pallas_call(
        _experts_body,
        grid_spec=pltpu.PrefetchScalarGridSpec(
            num_scalar_prefetch=3, grid=(n_tiles,),
            in_specs=[pl.BlockSpec(memory_space=pl.ANY), wspec(D, D_FF), wspec(D, D_FF), wspec(D_FF, D)],
            out_specs=pl.BlockSpec((EXPERT_TILE, D), lambda j, te, na, src: (j, 0)),
            scratch_shapes=[pltpu.VMEM((2, EXPERT_TILE, D), F32), pltpu.SemaphoreType.DMA((2,))]),
        out_shape=jax.ShapeDtypeStruct((n_tiles * EXPERT_TILE, D), F32),
        compiler_params=_cp("arbitrary"),
        name="experts",
    )(tile_expert, n_active, source, h, w_gate, w_up, w_down)


def _combine_body(slots_ref, y_hbm, x_ref, rt_ref, m_ref, o_ref, y_s, sem, *, n, tm):
    base = pl.program_id(0) * tm

    def row_copy(r, k):
        return pltpu.make_async_copy(y_hbm.at[pl.ds(slots_ref[k * n + base + r], 1)], y_s.at[k, pl.ds(r, 1)], sem)

    def issue(r, c):
        for k in range(TOP_K):
            row_copy(r, k).start()
        return c

    def drain(r, c):
        for k in range(TOP_K):
            row_copy(r, k).wait()
        return c

    lax.fori_loop(0, tm, issue, 0)
    lax.fori_loop(0, tm, drain, 0)
    rt = rt_ref[...]
    o_ref[...] = x_ref[...] + m_ref[5:6, :] * (rt[:, 4:5] * y_s[0] + rt[:, 5:6] * y_s[1])


def _combine(y, slots, x, rt, mods, l, cond_of_row):
    n = x.shape[0]
    tm = 256
    return pl.pallas_call(
        functools.partial(_combine_body, n=n, tm=tm),
        grid_spec=pltpu.PrefetchScalarGridSpec(
            num_scalar_prefetch=1, grid=(n // tm,),
            in_specs=[pl.BlockSpec(memory_space=pl.ANY),
                      pl.BlockSpec((tm, D), lambda i, s: (i, 0)),
                      pl.BlockSpec((tm, LANE), lambda i, s: (i, 0)),
                      pl.BlockSpec((None, None, N_MOD, D), lambda i, s: (l, cond_of_row(i * tm), 0, 0))],
            out_specs=pl.BlockSpec((tm, D), lambda i, s: (i, 0)),
            scratch_shapes=[pltpu.VMEM((TOP_K, tm, D), F32), pltpu.SemaphoreType.DMA(())]),
        out_shape=jax.ShapeDtypeStruct((n, D), F32),
        compiler_params=_cp("arbitrary"),
        name="combine",
    )(slots, y, x, rt, mods)


def _rope_tables(n_pos, n_identity):
    t = jnp.arange(n_pos)
    quarter = ROPE // 4
    inv = ROPE_BASE ** (-jnp.arange(quarter, dtype=F32) / quarter)
    ang_r = (t // GRID_W).astype(F32)[:, None] * inv
    ang_c = (t % GRID_W).astype(F32)[:, None] * inv
    one = jnp.ones((n_pos, LANE - ROPE), F32)
    cos = jnp.concatenate([jnp.cos(ang_r), jnp.cos(ang_r), jnp.cos(ang_c), jnp.cos(ang_c), one], axis=1)
    sin = jnp.concatenate([-jnp.sin(ang_r), jnp.sin(ang_r), -jnp.sin(ang_c), jnp.sin(ang_c), 0.0 * one], axis=1)
    cos = jnp.concatenate([cos, jnp.ones((n_identity, LANE), F32)], axis=0)
    sin = jnp.concatenate([sin, jnp.zeros((n_identity, LANE), F32)], axis=0)
    return cos, sin


def _pad_heads(w):
    lead = w.shape[:-1]
    w = w.reshape(lead + (H, QK_DIM))
    w = jnp.pad(w, [(0, 0)] * len(lead) + [(0, 0), (0, HQ - QK_DIM)])
    return w.reshape(lead + (H * HQ,))


def kernel(x_prompt, x_sample, cache_ckv, cache_krope, state_delta, c, c_ctx, g_mix, w_mod, b_mod, w_in, conv_w,
           a_log, dt_bias, g_dn_out, g_qa, w_uq, g_kva, w_ukv, g_qh, g_kh, w_out, g_ffn, w_router, b_router,
           w_gate, w_up, w_down):
    nb, seq, _ = x_prompt.shape
    ndb, dseq, _ = x_sample.shape
    n_ctx = nb * seq
    past = cache_ckv.shape[2]

    def cond_of_row(r):
        return jnp.where(r < n_ctx, 0, 1 + (r - n_ctx) // dseq)

    def pos_block(r, tm):
        return jnp.where(r < n_ctx, dseq // tm, ((r - n_ctx) % dseq) // tm)

    conds = jnp.concatenate([c_ctx[None, :], c, jnp.zeros((8 - 1 - ndb, D), F32)], axis=0)
    o_z = QKVZ
    o_a, o_b = o_z, o_z + 2 * H
    o_cq = o_b + 2 * H
    o_ckv = o_cq + Q_LORA
    o_kr = o_ckv + KV_LORA
    zc = lambda k: jnp.zeros((DEPTH, D, k), F32)
    w_tail = jnp.concatenate([w_in[:, :, o_cq:o_ckv], w_in[:, :, o_ckv:o_kr], w_in[:, :, o_kr:o_kr + ROPE],
                              zc(LANE - ROPE), w_in[:, :, o_a:o_cq], zc(LANE - 4 * H)], axis=2).astype(BF16)
    pad_lane = lambda v: jnp.pad(v.reshape(DEPTH, 1, -1), ((0, 0), (0, 0), (0, LANE - 2 * H)))
    alog_p = pad_lane(a_log)
    dtb_p = pad_lane(dt_bias)
    wq_p = _pad_heads(w_uq).astype(BF16)
    wkv_p = w_ukv.astype(BF16)
    gq_p = jnp.pad(g_qh, ((0, 0), (0, HQ - QK_DIM))).reshape(DEPTH, 1, HQ)
    gk_p = jnp.pad(g_kh, ((0, 0), (0, HQ - QK_DIM))).reshape(DEPTH, 1, HQ)
    g_qa_p = g_qa.reshape(DEPTH, 1, Q_LORA)
    g_kva_p = g_kva.reshape(DEPTH, 1, KV_LORA)
    cache_kr_p = jnp.pad(cache_krope, ((0, 0), (0, 0), (0, 0), (0, LANE - ROPE)))
    w_out_b = w_out.astype(BF16)
    wr_p = jnp.pad(w_router, ((0, 0), (0, LANE - N_EXP)))
    br_p = jnp.pad(b_router, (0, LANE - N_EXP)).reshape(1, LANE)
    cos_t, sin_t = _rope_tables(dseq, 512)
    s0_ctx = jnp.zeros((nb, 2, H, DK, DK), F32)

    mods = _modulation(conds, w_mod, b_mod)
    x = jnp.concatenate([x_prompt.reshape(n_ctx, D), x_sample.reshape(ndb * dseq, D)], axis=0)

    ckv_list, krope_list, state_list = [], [], []
    for l in range(DEPTH):
        proj = _in_proj(x, mods, g_mix, w_in, w_tail, l, cond_of_row)
        dn_ctx, s_ctx = _deltanet(proj, conv_w, alog_p, dtb_p, g_dn_out, s0_ctx, l, seq, 0, 8, 1)
        dn_lat, _ = _deltanet(proj, conv_w, alog_p, dtb_p, g_dn_out, state_delta[:, l], l, dseq, n_ctx, 2, 4)
        q, k, v, ckvn, krope = _mla_prep(proj, wq_p, wkv_p, g_qa_p, g_kva_p, gq_p, gk_p, cos_t, sin_t, l, pos_block)
        kc, vc = _cache_prep(cache_ckv, cache_kr_p, wkv_p, gk_p, l)
        mla_ctx = _attention(q, [(k, v, seq, 0)], nb, seq, 0, seq, H)
        mla_lat = _attention(q, [(kc, vc, past, 0), (k, v, dseq, n_ctx)], ndb, dseq, n_ctx, 512, 2)
        x = _out_proj(x, dn_ctx, mla_ctx, dn_lat, mla_lat, w_out_b, mods, l, cond_of_row)
        h, rt, cnt = _route(x, mods, g_ffn, wr_p, br_p, l, cond_of_row)
        slots, source, tile_expert, n_active, n_tiles = _slot_plan(rt, cnt, x.shape[0])
        y = _experts(h, source, tile_expert, n_active, w_gate, w_up, w_down, l, n_tiles)
        x = _combine(y, slots, x, rt, mods, l, cond_of_row)
        ckv_list.append(ckvn[:n_ctx].reshape(nb, seq, KV_LORA))
        krope_list.append(krope[:n_ctx].reshape(nb, seq, ROPE))
        state_list.append(s_ctx)

    y_prompt = x[:n_ctx].reshape(nb, seq, D)
    y_sample = x[n_ctx:].reshape(ndb, dseq, D)
    return (y_prompt, y_sample, jnp.stack(ckv_list, axis=1), jnp.stack(krope_list, axis=1),
            jnp.stack(state_list, axis=1))
```

```python
import functools

import jax
import jax.numpy as jnp
from jax import lax
from jax.experimental import pallas as pl
from jax.experimental.pallas import tpu as pltpu

F32 = jnp.float32
BF16 = jnp.bfloat16

D = 2048
DEPTH = 4
GRID_W = 64
H = 8
DK = 128
CONV_W = 5
CHUNK = 64
Q_LORA = 512
KV_LORA = 256
ROPE = 64
QK_DIM = DK + ROPE
ROPE_BASE = 10000.0
N_EXP = 16
D_FF = 512
N_MOD = 6
EPS = 1e-6
NEG = -1e30

LANE = 128
QKVZ = 4 * H * DK
TAIL = 1024
PROJ = QKVZ + TAIL
HQ = 2 * LANE

VMEM_LIMIT = 56 * 1024 * 1024


def _cp(*sem):
    return pltpu.CompilerParams(dimension_semantics=sem, vmem_limit_bytes=VMEM_LIMIT)


def _sigmoid(x):
    return 1.0 / (1.0 + jnp.exp(-x))


def _softplus(x):
    return jnp.maximum(x, 0.0) + jnp.log(1.0 + jnp.exp(-jnp.abs(x)))


def _mm(a, b):
    return jnp.dot(a.astype(BF16), b.astype(BF16), preferred_element_type=F32)


def _mm_f32(a, b):
    return jnp.dot(a, b, preferred_element_type=F32, precision=lax.Precision.HIGHEST)


def _mod_body(c_ref, w_ref, b_ref, o_ref):
    c = c_ref[...]
    o_ref[...] = _mm(c * _sigmoid(c), w_ref[...]) + b_ref[...]


def _modulation(conds, w_mod, b_mod):
    tn = 1024
    out = pl.pallas_call(
        _mod_body,
        grid=(DEPTH, N_MOD * D // tn),
        in_specs=[
            pl.BlockSpec((8, D), lambda l, j: (0, 0)),
            pl.BlockSpec((None, D, tn), lambda l, j: (l, 0, j)),
            pl.BlockSpec((None, 1, tn), lambda l, j: (l, 0, j)),
        ],
        out_specs=pl.BlockSpec((None, 8, tn), lambda l, j: (l, 0, j)),
        out_shape=jax.ShapeDtypeStruct((DEPTH, 8, N_MOD * D), F32),
        compiler_params=_cp("arbitrary", "arbitrary"),
        name="modulation",
    )(conds, w_mod, b_mod.reshape(DEPTH, 1, N_MOD * D))
    return out.reshape(DEPTH, 8, N_MOD, D)


def _in_body(x_ref, m_ref, g_ref, wm_ref, wt_ref, o_ref, h_scr, *, n_main):
    j = pl.program_id(1)

    @pl.when(j == 0)
    def _():
        x = x_ref[...]
        r = lax.rsqrt(jnp.mean(x * x, axis=-1, keepdims=True) + EPS)
        m = m_ref[...]
        h_scr[...] = (x * r * g_ref[...] * (1.0 + m[1:2, :]) + m[0:1, :]).astype(BF16)

    @pl.when(j < n_main)
    def _():
        o_ref[...] = jnp.dot(h_scr[...], wm_ref[...].astype(BF16), preferred_element_type=F32)

    @pl.when(j >= n_main)
    def _():
        o_ref[...] = jnp.dot(h_scr[...], wt_ref[...], preferred_element_type=F32)


def _in_proj(x, mods, g_mix, w_in, w_tail, l, cond_of_row):
    n = x.shape[0]
    tm, tn = 1024, 512
    n_main = QKVZ // tn
    n_tail = TAIL // tn
    return pl.pallas_call(
        functools.partial(_in_body, n_main=n_main),
        grid=(n // tm, n_main + n_tail),
        in_specs=[
            pl.BlockSpec((tm, D), lambda i, j: (i, 0)),
            pl.BlockSpec((None, None, N_MOD, D), lambda i, j: (l, cond_of_row(i * tm), 0, 0)),
            pl.BlockSpec((None, 1, D), lambda i, j: (l, 0, 0)),
            pl.BlockSpec((None, D, tn), lambda i, j: (l, 0, jnp.minimum(j, n_main - 1))),
            pl.BlockSpec((None, D, tn), lambda i, j: (l, 0, jnp.maximum(j - n_main, 0))),
        ],
        out_specs=pl.BlockSpec((tm, tn), lambda i, j: (i, j)),
        out_shape=jax.ShapeDtypeStruct((n, PROJ), F32),
        scratch_shapes=[pltpu.VMEM((tm, D), BF16)],
        compiler_params=_cp("arbitrary", "arbitrary"),
        name="in_proj",
    )(x, mods, g_mix.reshape(DEPTH, 1, D), w_in, w_tail)


def _bmm(a, b):
    return jnp.einsum('bij,bjk->bik', a.astype(BF16), b.astype(BF16), preferred_element_type=F32)


def _bmm_nt(a, b):
    return jnp.einsum('bid,bjd->bij', a.astype(BF16), b.astype(BF16), preferred_element_type=F32)


def _bmm_tn(a, b):
    return jnp.einsum('bci,bcj->bij', a.astype(BF16), b.astype(BF16), preferred_element_type=F32)


def _unit_triangular_inverse(lmat, ii, jj):
    b16 = (ii >> 4) == (jj >> 4)
    b32 = (ii >> 5) == (jj >> 5)
    eye = jnp.where(ii == jj, 1.0, 0.0)
    ld = jnp.where(b16, lmat, 0.0)
    l1 = jnp.where(b32, lmat - ld, 0.0)
    l2 = jnp.where(b32, 0.0, lmat)
    p = eye - ld
    a = _bmm(ld, ld)
    p = p + _bmm(p, a)
    a = _bmm(a, a)
    p = p + _bmm(p, a)
    a = _bmm(a, a)
    p = p + _bmm(p, a)
    t32 = p - _bmm(_bmm(p, l1), p)
    return t32 - _bmm(_bmm(t32, l2), t32)


def _dn_body(q_ref, k_ref, v_ref, z_ref, ab_ref, cwq_ref, cwk_ref, cwv_ref, alog_ref, dtb_ref, gout_ref, s0_ref,
             o_ref, sfin_ref, q_s, k_s, v_s, gc_s, beta_s, u_s, w_s, qd_s, kd_s, at_s, o_s, gate_s, *, t, nh, unroll):
    n_chunks = t // CHUNK
    span = unroll * CHUNK
    nb = nh * unroll
    row = lax.broadcasted_iota(jnp.int32, (t, LANE), 0)
    lane = lax.broadcasted_iota(jnp.int32, (t, LANE), 1)
    head0 = pl.program_id(1) * nh

    def conv_act(x, cw):
        acc = x * cw[2:3, :]
        for j in (0, 1, 3, 4):
            d = j - CONV_W // 2
            shifted = pltpu.roll(x, (-d) % t, axis=0)
            ok = jnp.logical_and(row + d >= 0, row + d < t)
            acc = acc + jnp.where(ok, shifted, 0.0) * cw[j:j + 1, :]
        return acc * _sigmoid(acc)

    def l2n(x):
        return x * lax.rsqrt(jnp.sum(x * x, axis=-1, keepdims=True) + EPS)

    @pl.when(pl.program_id(1) == 0)
    def _():
        ab = ab_ref[...]
        g_all = -jnp.exp(alog_ref[...]) * _softplus(ab + dtb_ref[...])
        pre = g_all
        suf = g_all
        r_in = row & (CHUNK - 1)
        s = 1
        while s < CHUNK:
            pre = pre + jnp.where(r_in >= s, pltpu.roll(pre, s, axis=0), 0.0)
            suf = suf + jnp.where(r_in < CHUNK - s, pltpu.roll(suf, t - s, axis=0), 0.0)
            s *= 2
        gate_s[0] = pre
        gate_s[1] = suf
        gate_s[2] = _sigmoid(ab)

    pre = gate_s[0]
    suf = gate_s[1]
    beta_all = gate_s[2]

    def column(x, idx):
        c = jnp.sum(jnp.where(lane == idx, x, 0.0), axis=1, keepdims=True)
        return jnp.broadcast_to(c, (t, LANE))

    for h in range(nh):
        sl = slice(h * LANE, (h + 1) * LANE)
        q_s[h] = l2n(conv_act(q_ref[:, sl], cwq_ref[:, sl])) * (DK ** -0.5)
        k_s[h] = l2n(conv_act(k_ref[:, sl], cwk_ref[:, sl]))
        v_s[h] = conv_act(v_ref[:, sl], cwv_ref[:, sl])
        gc_s[0, h] = column(pre, head0 + h)
        gc_s[1, h] = column(suf, H + head0 + h)
        beta_s[0, h] = column(beta_all, 2 * H + head0 + h)
        beta_s[1, h] = column(beta_all, 3 * H + head0 + h)
    o_s[...] = jnp.zeros_like(o_s)

    ii = lax.broadcasted_iota(jnp.int32, (1, CHUNK, CHUNK), 1)
    jj = lax.broadcasted_iota(jnp.int32, (1, CHUNK, CHUNK), 2)
    causal = ((ii >= jj, ii > jj), (ii <= jj, ii < jj))

    def chunk_end_gate(gc, d):
        return gc[:, CHUNK - 1:CHUNK, :] if d == 0 else gc[:, 0:1, :]

    def prepare_group(g, carry):
        rows = pl.ds(pl.multiple_of(g * span, span), span)
        per_chunk = lambda x: x.reshape(nb, CHUNK, x.shape[-1])
        per_head = lambda x: x.reshape(nh, span, x.shape[-1])
        q = per_chunk(q_s[:, rows, :])
        k = per_chunk(k_s[:, rows, :])
        v = per_chunk(v_s[:, rows, :])
        kk = _bmm_nt(k, k)
        qk = _bmm_nt(q, k)
        lmats, rhs = [], []
        for d in (0, 1):
            incl, strict = causal[d]
            gc = per_chunk(gc_s[d, :, rows, :])
            beta = per_chunk(beta_s[d, :, rows, :])
            gc_c = gc[:, :, :CHUNK]
            gc_row = jnp.sum(jnp.where(ii == jj, gc_c, 0.0), axis=1, keepdims=True)
            decay = jnp.exp(jnp.where(incl, gc_c - gc_row, NEG))
            e_gc = jnp.exp(gc)
            lmats.append(jnp.where(strict, kk * beta[:, :, :CHUNK] * decay, 0.0))
            rhs.append(jnp.concatenate([v * beta, k * beta * e_gc], axis=-1))
            qd_s[d, :, rows, :] = per_head(q * e_gc).astype(BF16)
            kd_s[d, :, rows, :] = per_head(k * jnp.exp(chunk_end_gate(gc, d) - gc)).astype(BF16)
            at_s[d, :, rows, :] = per_head(jnp.where(incl, qk * decay, 0.0)).astype(BF16)
        tmat = _unit_triangular_inverse(jnp.concatenate(lmats, axis=0), ii, jj)
        uw = _bmm(tmat, jnp.concatenate(rhs, axis=0))
        for d in (0, 1):
            part = uw[d * nb:(d + 1) * nb]
            u_s[d, :, rows, :] = per_head(part[:, :, :LANE])
            w_s[d, :, rows, :] = per_head(part[:, :, LANE:]).astype(BF16)
        return carry

    lax.fori_loop(0, n_chunks // unroll, prepare_group, 0)

    def scan_step(i, states):
        rows = [pl.ds(pl.multiple_of(c * CHUNK, CHUNK), CHUNK) for c in (i, n_chunks - 1 - i)]
        r = [_bmm(jnp.concatenate([w_s[d, :, rows[d], :], qd_s[d, :, rows[d], :]], axis=1), states[d])
             for d in (0, 1)]
        v_new = [(u_s[d, :, rows[d], :] - r[d][:, :CHUNK, :]).astype(BF16) for d in (0, 1)]
        o_add = [r[d][:, CHUNK:, :] + _bmm(at_s[d, :, rows[d], :], v_new[d]) for d in (0, 1)]
        s_add = [_bmm_tn(kd_s[d, :, rows[d], :], v_new[d]) for d in (0, 1)]
        out = []
        for d in (0, 1):
            o_s[:, rows[d], :] += o_add[d]
            out.append(states[d] * jnp.exp(chunk_end_gate(gc_s[d, :, rows[d], :], d)) + s_add[d])
        return tuple(out)

    sf, sb = lax.fori_loop(0, n_chunks, scan_step, (s0_ref[0], s0_ref[1]))
    sfin_ref[0] = sf
    sfin_ref[1] = sb

    for h in range(nh):
        sl = slice(h * LANE, (h + 1) * LANE)
        o = o_s[h]
        y = o * lax.rsqrt(jnp.mean(o * o, axis=-1, keepdims=True) + EPS) * gout_ref[...]
        z = z_ref[:, sl]
        o_ref[:, sl] = (y * z * _sigmoid(z)).astype(BF16)


def _deltanet(proj, conv_w, alog, dtb, g_out, s0, l, t, row0, nh, unroll):
    nseq = s0.shape[0]
    rb0 = row0 // t
    w = nh * LANE
    ng = H // nh
    tok = lambda part: pl.BlockSpec((t, w), lambda s, g: (rb0 + s, part * ng + g))
    cw = lambda part: pl.BlockSpec((None, CONV_W, w), lambda s, g: (l, 0, part * ng + g))
    small = pl.BlockSpec((None, 1, LANE), lambda s, g: (l, 0, 0))
    st = pl.BlockSpec((None, 2, nh, DK, DK), lambda s, g: (s, 0, g, 0, 0))
    wide = lambda dt: pltpu.VMEM((2, nh, t, LANE), dt)
    return pl.pallas_call(
        functools.partial(_dn_body, t=t, nh=nh, unroll=unroll),
        grid=(nseq, ng),
        in_specs=[tok(0), tok(1), tok(2), tok(3),
                  pl.BlockSpec((t, LANE), lambda s, g: (rb0 + s, PROJ // LANE - 1)),
                  cw(0), cw(1), cw(2), small, small, small, st],
        out_specs=[pl.BlockSpec((t, w), lambda s, g: (s, g)), st],
        out_shape=[jax.ShapeDtypeStruct((nseq * t, H * DK), BF16),
                   jax.ShapeDtypeStruct((nseq, 2, H, DK, DK), F32)],
        scratch_shapes=[pltpu.VMEM((nh, t, LANE), F32), pltpu.VMEM((nh, t, LANE), F32), pltpu.VMEM((nh, t, LANE), F32),
                        wide(F32), wide(F32), wide(F32), wide(BF16), wide(BF16), wide(BF16),
                        pltpu.VMEM((2, nh, t, CHUNK), BF16), pltpu.VMEM((nh, t, LANE), F32),
                        pltpu.VMEM((3, t, LANE), F32)],
        compiler_params=_cp("arbitrary", "arbitrary"),
        name="deltanet",
    )(proj, proj, proj, proj, proj, conv_w, conv_w, conv_w, alog, dtb, g_out.reshape(DEPTH, 1, DK), s0)


def _rope_swap(x, lane):
    return jnp.where((lane & 31) < 16, pltpu.roll(x, LANE - 16, axis=1), pltpu.roll(x, 16, axis=1))


def _head_keys_values(kv, kr, gk, cos, sin, lane, k_ref, v_ref):
    kr_ss = jnp.sum(kr * kr, axis=-1, keepdims=True)
    for h in range(H):
        kn = kv[:, h * HQ:h * HQ + DK]
        r = lax.rsqrt((jnp.sum(kn * kn, axis=-1, keepdims=True) + kr_ss) * (1.0 / QK_DIM) + EPS)
        k_ref[:, h * HQ:h * HQ + DK] = (kn * r * gk[:, :DK]).astype(BF16)
        rr = kr * r * gk[:, DK:]
        if cos is not None:
            rr = rr * cos + _rope_swap(rr, lane) * sin
        k_ref[:, h * HQ + DK:(h + 1) * HQ] = rr.astype(BF16)
        v_ref[:, h * DK:(h + 1) * DK] = kv[:, h * HQ + DK:(h + 1) * HQ].astype(BF16)


def _mla_prep_body(cq_ref, ckv_ref, kr_ref, wq_ref, wkv_ref, gqa_ref, gkva_ref, gq_ref, gk_ref, cos_ref, sin_ref,
                   q_ref, k_ref, v_ref, ckvn_ref, krope_ref):
    tm = cq_ref.shape[0]
    lane = lax.broadcasted_iota(jnp.int32, (tm, LANE), 1)
    cos = cos_ref[...]
    sin = sin_ref[...]
    cq = cq_ref[...]
    cqn = cq * lax.rsqrt(jnp.mean(cq * cq, axis=-1, keepdims=True) + EPS) * gqa_ref[...]
    q = jnp.dot(cqn.astype(BF16), wq_ref[...], preferred_element_type=F32)
    gq = gq_ref[...]
    for h in range(H):
        qn = q[:, h * HQ:h * HQ + DK]
        qr = q[:, h * HQ + DK:(h + 1) * HQ]
        ss = jnp.sum(qn * qn, axis=-1, keepdims=True) + jnp.sum(qr * qr, axis=-1, keepdims=True)
        r = lax.rsqrt(ss * (1.0 / QK_DIM) + EPS) * (QK_DIM ** -0.5)
        q_ref[:, h * HQ:h * HQ + DK] = (qn * r * gq[:, :DK]).astype(BF16)
        qr = qr * r * gq[:, DK:]
        q_ref[:, h * HQ + DK:(h + 1) * HQ] = (qr * cos + _rope_swap(qr, lane) * sin).astype(BF16)
    ckv = ckv_ref[...]
    ckvn = ckv * lax.rsqrt(jnp.mean(ckv * ckv, axis=-1, keepdims=True) + EPS) * gkva_ref[...]
    ckvn_ref[...] = ckvn
    kr = kr_ref[...]
    krope_ref[...] = kr[:, :ROPE]
    kv = jnp.dot(ckvn.astype(BF16), wkv_ref[...], preferred_element_type=F32)
    _head_keys_values(kv, kr, gk_ref[...], cos, sin, lane, k_ref, v_ref)


def _mla_prep(proj, wq, wkv, g_qa, g_kva, gq, gk, cos_t, sin_t, l, pos_block):
    n = proj.shape[0]
    tm = 512
    lw = lambda shape: pl.BlockSpec((None,) + shape, lambda i: (l,) + (0,) * len(shape))
    table = pl.BlockSpec((tm, LANE), lambda i: (pos_block(i * tm, tm), 0))
    return pl.pallas_call(
        _mla_prep_body,
        grid=(n // tm,),
        in_specs=[pl.BlockSpec((tm, Q_LORA), lambda i: (i, QKVZ // Q_LORA)),
                  pl.BlockSpec((tm, KV_LORA), lambda i: (i, (QKVZ + Q_LORA) // KV_LORA)),
                  pl.BlockSpec((tm, LANE), lambda i: (i, (QKVZ + Q_LORA + KV_LORA) // LANE)),
                  lw((Q_LORA, H * HQ)), lw((KV_LORA, H * HQ)), lw((1, Q_LORA)), lw((1, KV_LORA)),
                  lw((1, HQ)), lw((1, HQ)), table, table],
        out_specs=[pl.BlockSpec((tm, H * HQ), lambda i: (i, 0)), pl.BlockSpec((tm, H * HQ), lambda i: (i, 0)),
                   pl.BlockSpec((tm, H * DK), lambda i: (i, 0)), pl.BlockSpec((tm, KV_LORA), lambda i: (i, 0)),
                   pl.BlockSpec((tm, ROPE), lambda i: (i, 0))],
        out_shape=[jax.ShapeDtypeStruct((n, H * HQ), BF16), jax.ShapeDtypeStruct((n, H * HQ), BF16),
                   jax.ShapeDtypeStruct((n, H * DK), BF16), jax.ShapeDtypeStruct((n, KV_LORA), F32),
                   jax.ShapeDtypeStruct((n, ROPE), F32)],
        compiler_params=_cp("arbitrary"),
        name="mla_prep",
    )(proj, proj, proj, wq, wkv, g_qa, g_kva, gq, gk, cos_t, sin_t)


def _cache_prep_body(ckv_ref, kr_ref, wkv_ref, gk_ref, k_ref, v_ref):
    kv = jnp.dot(ckv_ref[...].astype(BF16), wkv_ref[...], preferred_element_type=F32)
    _head_keys_values(kv, kr_ref[...], gk_ref[...], None, None, None, k_ref, v_ref)


def _cache_prep(cache_ckv, cache_kr, wkv, gk, l):
    nb, _, past, _ = cache_ckv.shape
    return pl.pallas_call(
        _cache_prep_body,
        grid=(nb,),
        in_specs=[pl.BlockSpec((None, None, past, KV_LORA), lambda b: (b, l, 0, 0)),
                  pl.BlockSpec((None, None, past, LANE), lambda b: (b, l, 0, 0)),
                  pl.BlockSpec((None, KV_LORA, H * HQ), lambda b: (l, 0, 0)),
                  pl.BlockSpec((None, 1, HQ), lambda b: (l, 0, 0))],
        out_specs=[pl.BlockSpec((past, H * HQ), lambda b: (b, 0)), pl.BlockSpec((past, H * DK), lambda b: (b, 0))],
        out_shape=[jax.ShapeDtypeStruct((nb * past, H * HQ), BF16), jax.ShapeDtypeStruct((nb * past, H * DK), BF16)],
        compiler_params=_cp("arbitrary"),
        name="cache_prep",
    )(cache_ckv, cache_kr, wkv, gk)


def _attn_body(*refs, nparts, hb):
    q_ref = refs[0]
    k_refs = refs[1:1 + nparts]
    v_refs = refs[1 + nparts:1 + 2 * nparts]
    o_ref = refs[-1]
    for h in range(hb):
        q = q_ref[:, h * HQ:(h + 1) * HQ]
        scores = [lax.dot_general(q, k[:, h * HQ:(h + 1) * HQ], (((1,), (1,)), ((), ())), preferred_element_type=F32)
                  for k in k_refs]
        m = jnp.max(scores[0], axis=-1, keepdims=True)
        for sc in scores[1:]:
            m = jnp.maximum(m, jnp.max(sc, axis=-1, keepdims=True))
        num = 0.0
        den = 0.0
        for sc, v in zip(scores, v_refs):
            p = jnp.exp(sc - m)
            den = den + jnp.sum(p, axis=-1, keepdims=True)
            num = num + jnp.dot(p.astype(BF16), v[:, h * DK:(h + 1) * DK], preferred_element_type=F32)
        o_ref[:, h * DK:(h + 1) * DK] = (num / den).astype(BF16)


def _attention(q, parts, nseq, t, row0, tq, hb):
    nparts = len(parts)
    qb0 = row0 // tq
    nq = t // tq
    k_specs, v_specs, ks, vs = [], [], [], []
    for k_arr, v_arr, s_len, k_row0 in parts:
        kb0 = k_row0 // s_len
        k_specs.append(pl.BlockSpec((s_len, hb * HQ), lambda s, h, i, kb0=kb0: (kb0 + s, h)))
        v_specs.append(pl.BlockSpec((s_len, hb * DK), lambda s, h, i, kb0=kb0: (kb0 + s, h)))
        ks.append(k_arr)
        vs.append(v_arr)
    return pl.pallas_call(
        functools.partial(_attn_body, nparts=nparts, hb=hb),
        grid=(nseq, H // hb, nq),
        in_specs=[pl.BlockSpec((tq, hb * HQ), lambda s, h, i: (qb0 + s * nq + i, h))] + k_specs + v_specs,
        out_specs=pl.BlockSpec((tq, hb * DK), lambda s, h, i: (s * nq + i, h)),
        out_shape=jax.ShapeDtypeStruct((nseq * t, H * DK), BF16),
        compiler_params=_cp("arbitrary", "arbitrary", "arbitrary"),
        name="attention",
    )(q, *ks, *vs)


def _out_body(x_ref, dnc_ref, mlac_ref, dnl_ref, mlal_ref, w_ref, m_ref, o_ref, *, ctx_tiles):
    w = w_ref[...]

    def emit(dn_ref, mla_ref):
        y = jnp.dot(dn_ref[...], w[:H * DK, :], preferred_element_type=F32)
        y = y + jnp.dot(mla_ref[...], w[H * DK:, :], preferred_element_type=F32)
        o_ref[...] = x_ref[...] + m_ref[2:3, :] * y

    @pl.when(pl.program_id(0) < ctx_tiles)
    def _():
        emit(dnc_ref, mlac_ref)

    @pl.when(pl.program_id(0) >= ctx_tiles)
    def _():
        emit(dnl_ref, mlal_ref)


def _out_proj(x, dn_ctx, mla_ctx, dn_lat, mla_lat, w_out, mods, l, cond_of_row):
    n = x.shape[0]
    tm, tn = 512, 512
    ctx_tiles = dn_ctx.shape[0] // tm
    ctx = pl.BlockSpec((tm, H * DK), lambda i, j: (jnp.minimum(i, ctx_tiles - 1), 0))
    lat = pl.BlockSpec((tm, H * DK), lambda i, j: (jnp.maximum(i - ctx_tiles, 0), 0))
    return pl.pallas_call(
        functools.partial(_out_body, ctx_tiles=ctx_tiles),
        grid=(n // tm, D // tn),
        in_specs=[pl.BlockSpec((tm, tn), lambda i, j: (i, j)), ctx, ctx, lat, lat,
                  pl.BlockSpec((None, 2 * H * DK, tn), lambda i, j: (l, 0, j)),
                  pl.BlockSpec((None, None, N_MOD, tn), lambda i, j: (l, cond_of_row(i * tm), 0, j))],
        out_specs=pl.BlockSpec((tm, tn), lambda i, j: (i, j)),
        out_shape=jax.ShapeDtypeStruct((n, D), F32),
        compiler_params=_cp("arbitrary", "arbitrary"),
        name="out_proj",
    )(x, dn_ctx, mla_ctx, dn_lat, mla_lat, w_out, mods)


def _group_lane(x, k, lane):
    return jnp.where((lane & 3) + k < 4, pltpu.roll(x, LANE - k, axis=1), pltpu.roll(x, 4 - k, axis=1))


def _route_body(x_ref, m_ref, g_ref, wr_ref, br_ref, h_ref, rt_ref, cnt_ref, run_s):
    @pl.when(pl.program_id(0) == 0)
    def _():
        run_s[...] = jnp.zeros_like(run_s)

    x = x_ref[...]
    m = m_ref[...]
    h = x * lax.rsqrt(jnp.mean(x * x, axis=-1, keepdims=True) + EPS) * g_ref[...] * (1.0 + m[4:5, :]) + m[3:4, :]
    h_ref[...] = h
    tm = x.shape[0]
    lane = lax.broadcasted_iota(jnp.int32, (tm, LANE), 1)
    valid = lane < N_EXP
    scores = _sigmoid(_mm_f32(h, wr_ref[...]))
    sel = jnp.where(valid, scores + br_ref[...], NEG)
    rank = jnp.zeros((tm, LANE), F32)
    for k in (1, 2, 3):
        other = _group_lane(sel, k, lane)
        other_first = (lane & 3) + k >= 4
        beats = jnp.logical_or(other > sel, jnp.logical_and(other == sel, other_first))
        rank = rank + jnp.where(beats, 1.0, 0.0)
    top2 = rank < 2.0
    t = jnp.where(top2, sel, 0.0)
    gscore = t + _group_lane(t, 1, lane) + _group_lane(t, 2, lane) + _group_lane(t, 3, lane)
    lost = jnp.zeros((tm, LANE), F32)
    for k in (1, 2, 3):
        wrapped = lane + 4 * k >= N_EXP
        other = jnp.where(wrapped, pltpu.roll(gscore, N_EXP - 4 * k, axis=1), pltpu.roll(gscore, LANE - 4 * k, axis=1))
        loses = jnp.logical_or(other > gscore, jnp.logical_and(other == gscore, wrapped))
        lost = lost + jnp.where(loses, 1.0, 0.0)
    chosen = jnp.logical_and(jnp.logical_and(lost == 0.0, top2), valid)
    num = jnp.where(chosen, scores, 0.0)
    gate = num / jnp.sum(num, axis=-1, keepdims=True)
    member = jnp.where(chosen, 1.0, 0.0)
    r_i = lax.broadcasted_iota(jnp.int32, (tm, tm), 0)
    c_i = lax.broadcasted_iota(jnp.int32, (tm, tm), 1)
    earlier = jnp.where(r_i > c_i, 1.0, 0.0)
    pos = _mm(earlier, member) + run_s[0:1, :]
    run_s[...] = run_s[...] + jnp.sum(member, axis=0, keepdims=True)
    cnt_ref[...] = run_s[...]
    lane_f = lane.astype(F32)
    rec = jnp.zeros((tm, LANE), F32)
    for k in (0, 1):
        pick = jnp.logical_and(chosen, rank == float(k))
        for field, val in ((0, lane_f), (2, pos), (4, gate)):
            col = jnp.sum(jnp.where(pick, val, 0.0), axis=-1, keepdims=True)
            rec = jnp.where(lane == field + k, col, rec)
    rt_ref[...] = rec


def _route(x, mods, g_ffn, w_router, b_router, l, cond_of_row):
    n = x.shape[0]
    tm = 512
    return pl.pallas_call(
        _route_body,
        grid=(n // tm,),
        in_specs=[pl.BlockSpec((tm, D), lambda i: (i, 0)),
                  pl.BlockSpec((None, None, N_MOD, D), lambda i: (l, cond_of_row(i * tm), 0, 0)),
                  pl.BlockSpec((None, 1, D), lambda i: (l, 0, 0)),
                  pl.BlockSpec((D, LANE), lambda i: (0, 0)),
                  pl.BlockSpec((1, LANE), lambda i: (0, 0))],
        out_specs=[pl.BlockSpec((tm, D), lambda i: (i, 0)), pl.BlockSpec((tm, LANE), lambda i: (i, 0)),
                   pl.BlockSpec((8, LANE), lambda i: (0, 0))],
        out_shape=[jax.ShapeDtypeStruct((n, D), F32), jax.ShapeDtypeStruct((n, LANE), F32),
                   jax.ShapeDtypeStruct((8, LANE), F32)],
        scratch_shapes=[pltpu.VMEM((8, LANE), F32)],
        compiler_params=_cp("arbitrary"),
        name="route",
    )(x, mods, g_ffn.reshape(DEPTH, 1, D), w_router, b_router)


EXPERT_TILE = 512
TOP_K = 2


def _slot_plan(rt, cnt, n):
    n_tiles = (TOP_K * n) // EXPERT_TILE + N_EXP
    counts = cnt[0, :N_EXP].astype(jnp.int32)
    padded = ((counts + EXPERT_TILE - 1) // EXPERT_TILE) * EXPERT_TILE
    ends = jnp.cumsum(padded)
    starts = ends - padded
    expert = rt[:, 0:TOP_K].astype(jnp.int32)
    slots = (starts[expert] + rt[:, 2:2 + TOP_K].astype(jnp.int32)).T.reshape(-1)
    n_active = (ends[-1] // EXPERT_TILE).reshape(1)
    tile_expert = jnp.minimum(jnp.searchsorted(ends, jnp.arange(n_tiles) * EXPERT_TILE, side="right"), N_EXP - 1)
    token = jnp.tile(jnp.arange(n, dtype=jnp.int32), TOP_K)
    source = jnp.zeros((n_tiles * EXPERT_TILE,), jnp.int32).at[slots].set(token)
    return slots.astype(jnp.int32), source, tile_expert.astype(jnp.int32), n_active.astype(jnp.int32), n_tiles


def _experts_body(te_ref, na_ref, src_ref, h_hbm, wg_ref, wu_ref, wd_ref, y_ref, x_s, sem):
    j = pl.program_id(0)
    n_active = na_ref[0]

    def gather_start(tile):
        buf = tile & 1
        base = tile * EXPERT_TILE

        def issue(r, c):
            src = h_hbm.at[pl.ds(src_ref[base + r], 1)]
            pltpu.make_async_copy(src, x_s.at[buf, pl.ds(r, 1)], sem.at[buf]).start()
            return c
        lax.fori_loop(0, EXPERT_TILE, issue, 0, unroll=8)

    def gather_wait(tile):
        buf = tile & 1
        pltpu.make_async_copy(h_hbm.at[pl.ds(0, EXPERT_TILE)], x_s.at[buf], sem.at[buf]).wait()

    @pl.when(j == 0)
    def _():
        gather_start(j)

    @pl.when(j + 1 < n_active)
    def _():
        gather_start(j + 1)

    @pl.when(j < n_active)
    def _():
        gather_wait(j)
        x = x_s[j & 1].astype(BF16)
        hg = jnp.dot(x, wg_ref[...].astype(BF16), preferred_element_type=F32)
        hu = jnp.dot(x, wu_ref[...].astype(BF16), preferred_element_type=F32)
        act = (hg * _sigmoid(hg) * hu).astype(BF16)
        y_ref[...] = jnp.dot(act, wd_ref[...].astype(BF16), preferred_element_type=F32)

    @pl.when(j >= n_active)
    def _():
        y_ref[...] = jnp.zeros_like(y_ref)


def _experts(h, source, tile_expert, n_active, w_gate, w_up, w_down, l, n_tiles):
    wspec = lambda a, b: pl.BlockSpec((None, None, a, b), lambda j, te, na, src: (l, te[j], 0, 0))
    return pl.pallas_call(
        _experts_body,
        grid_spec=pltpu.PrefetchScalarGridSpec(
            num_scalar_prefetch=3, grid=(n_tiles,),
            in_specs=[pl.BlockSpec(memory_space=pl.ANY), wspec(D, D_FF), wspec(D, D_FF), wspec(D_FF, D)],
            out_specs=pl.BlockSpec((EXPERT_TILE, D), lambda j, te, na, src: (j, 0)),
            scratch_shapes=[pltpu.VMEM((2, EXPERT_TILE, D), F32), pltpu.SemaphoreType.DMA((2,))]),
        out_shape=jax.ShapeDtypeStruct((n_tiles * EXPERT_TILE, D), F32),
        compiler_params=_cp("arbitrary"),
        name="experts",
    )(tile_expert, n_active, source, h, w_gate, w_up, w_down)


def _combine_body(slots_ref, y_hbm, x_ref, rt_ref, m_ref, o_ref, y_s, sem, *, n, tm):
    base = pl.program_id(0) * tm

    def issue(r, c):
        for k in range(TOP_K):
            src = y_hbm.at[pl.ds(slots_ref[k * n + base + r], 1)]
            pltpu.make_async_copy(src, y_s.at[k, pl.ds(r, 1)], sem).start()
        return c

    lax.fori_loop(0, tm, issue, 0, unroll=8)
    for k in range(TOP_K):
        pltpu.make_async_copy(y_hbm.at[pl.ds(0, tm)], y_s.at[k], sem).wait()
    rt = rt_ref[...]
    o_ref[...] = x_ref[...] + m_ref[5:6, :] * (rt[:, 4:5] * y_s[0] + rt[:, 5:6] * y_s[1])


def _combine(y, slots, x, rt, mods, l, cond_of_row):
    n = x.shape[0]
    tm = 256
    return pl.pallas_call(
        functools.partial(_combine_body, n=n, tm=tm),
        grid_spec=pltpu.PrefetchScalarGridSpec(
            num_scalar_prefetch=1, grid=(n // tm,),
            in_specs=[pl.BlockSpec(memory_space=pl.ANY),
                      pl.BlockSpec((tm, D), lambda i, s: (i, 0)),
                      pl.BlockSpec((tm, LANE), lambda i, s: (i, 0)),
                      pl.BlockSpec((None, None, N_MOD, D), lambda i, s: (l, cond_of_row(i * tm), 0, 0))],
            out_specs=pl.BlockSpec((tm, D), lambda i, s: (i, 0)),
            scratch_shapes=[pltpu.VMEM((TOP_K, tm, D), F32), pltpu.SemaphoreType.DMA(())]),
        out_shape=jax.ShapeDtypeStruct((n, D), F32),
        compiler_params=_cp("arbitrary"),
        name="combine",
    )(slots, y, x, rt, mods)


def _rope_tables(n_pos, n_identity):
    t = jnp.arange(n_pos)
    quarter = ROPE // 4
    inv = ROPE_BASE ** (-jnp.arange(quarter, dtype=F32) / quarter)
    ang_r = (t // GRID_W).astype(F32)[:, None] * inv
    ang_c = (t % GRID_W).astype(F32)[:, None] * inv
    one = jnp.ones((n_pos, LANE - ROPE), F32)
    cos = jnp.concatenate([jnp.cos(ang_r), jnp.cos(ang_r), jnp.cos(ang_c), jnp.cos(ang_c), one], axis=1)
    sin = jnp.concatenate([-jnp.sin(ang_r), jnp.sin(ang_r), -jnp.sin(ang_c), jnp.sin(ang_c), 0.0 * one], axis=1)
    cos = jnp.concatenate([cos, jnp.ones((n_identity, LANE), F32)], axis=0)
    sin = jnp.concatenate([sin, jnp.zeros((n_identity, LANE), F32)], axis=0)
    return cos, sin


def _pad_heads(w):
    lead = w.shape[:-1]
    w = w.reshape(lead + (H, QK_DIM))
    w = jnp.pad(w, [(0, 0)] * len(lead) + [(0, 0), (0, HQ - QK_DIM)])
    return w.reshape(lead + (H * HQ,))


def kernel(x_prompt, x_sample, cache_ckv, cache_krope, state_delta, c, c_ctx, g_mix, w_mod, b_mod, w_in, conv_w,
           a_log, dt_bias, g_dn_out, g_qa, w_uq, g_kva, w_ukv, g_qh, g_kh, w_out, g_ffn, w_router, b_router,
           w_gate, w_up, w_down):
    nb, seq, _ = x_prompt.shape
    ndb, dseq, _ = x_sample.shape
    n_ctx = nb * seq
    past = cache_ckv.shape[2]

    def cond_of_row(r):
        return jnp.where(r < n_ctx, 0, 1 + (r - n_ctx) // dseq)

    def pos_block(r, tm):
        return jnp.where(r < n_ctx, dseq // tm, ((r - n_ctx) % dseq) // tm)

    conds = jnp.concatenate([c_ctx[None, :], c, jnp.zeros((8 - 1 - ndb, D), F32)], axis=0)
    o_z = QKVZ
    o_a, o_b = o_z, o_z + 2 * H
    o_cq = o_b + 2 * H
    o_ckv = o_cq + Q_LORA
    o_kr = o_ckv + KV_LORA
    zc = lambda k: jnp.zeros((DEPTH, D, k), F32)
    w_tail = jnp.concatenate([w_in[:, :, o_cq:o_ckv], w_in[:, :, o_ckv:o_kr], w_in[:, :, o_kr:o_kr + ROPE],
                              zc(LANE - ROPE), w_in[:, :, o_a:o_cq], zc(LANE - 4 * H)], axis=2).astype(BF16)
    pad_lane = lambda v: jnp.pad(v.reshape(DEPTH, 1, -1), ((0, 0), (0, 0), (0, LANE - 2 * H)))
    alog_p = pad_lane(a_log)
    dtb_p = pad_lane(dt_bias)
    wq_p = _pad_heads(w_uq).astype(BF16)
    wkv_p = w_ukv.astype(BF16)
    gq_p = jnp.pad(g_qh, ((0, 0), (0, HQ - QK_DIM))).reshape(DEPTH, 1, HQ)
    gk_p = jnp.pad(g_kh, ((0, 0), (0, HQ - QK_DIM))).reshape(DEPTH, 1, HQ)
    g_qa_p = g_qa.reshape(DEPTH, 1, Q_LORA)
    g_kva_p = g_kva.reshape(DEPTH, 1, KV_LORA)
    cache_kr_p = jnp.pad(cache_krope, ((0, 0), (0, 0), (0, 0), (0, LANE - ROPE)))
    w_out_b = w_out.astype(BF16)
    wr_p = jnp.pad(w_router, ((0, 0), (0, LANE - N_EXP)))
    br_p = jnp.pad(b_router, (0, LANE - N_EXP)).reshape(1, LANE)
    cos_t, sin_t = _rope_tables(dseq, 512)
    s0_ctx = jnp.zeros((nb, 2, H, DK, DK), F32)

    mods = _modulation(conds, w_mod, b_mod)
    x = jnp.concatenate([x_prompt.reshape(n_ctx, D), x_sample.reshape(ndb * dseq, D)], axis=0)

    ckv_list, krope_list, state_list = [], [], []
    for l in range(DEPTH):
        proj = _in_proj(x, mods, g_mix, w_in, w_tail, l, cond_of_row)
        dn_ctx, s_ctx = _deltanet(proj, conv_w, alog_p, dtb_p, g_dn_out, s0_ctx, l, seq, 0, 8, 1)
        dn_lat, _ = _deltanet(proj, conv_w, alog_p, dtb_p, g_dn_out, state_delta[:, l], l, dseq, n_ctx, 2, 4)
        q, k, v, ckvn, krope = _mla_prep(proj, wq_p, wkv_p, g_qa_p, g_kva_p, gq_p, gk_p, cos_t, sin_t, l, pos_block)
        kc, vc = _cache_prep(cache_ckv, cache_kr_p, wkv_p, gk_p, l)
        mla_ctx = _attention(q, [(k, v, seq, 0)], nb, seq, 0, seq, H)
        mla_lat = _attention(q, [(kc, vc, past, 0), (k, v, dseq, n_ctx)], ndb, dseq, n_ctx, 512, 2)
        x = _out_proj(x, dn_ctx, mla_ctx, dn_lat, mla_lat, w_out_b, mods, l, cond_of_row)
        h, rt, cnt = _route(x, mods, g_ffn, wr_p, br_p, l, cond_of_row)
        slots, source, tile_expert, n_active, n_tiles = _slot_plan(rt, cnt, x.shape[0])
        y = _experts(h, source, tile_expert, n_active, w_gate, w_up, w_down, l, n_tiles)
        x = _combine(y, slots, x, rt, mods, l, cond_of_row)
        ckv_list.append(ckvn[:n_ctx].reshape(nb, seq, KV_LORA))
        krope_list.append(krope[:n_ctx].reshape(nb, seq, ROPE))
        state_list.append(s_ctx)

    y_prompt = x[:n_ctx].reshape(nb, seq, D)
    y_sample = x[n_ctx:].reshape(ndb, dseq, D)
    return (y_prompt, y_sample, jnp.stack(ckv_list, axis=1), jnp.stack(krope_list, axis=1),
            jnp.stack(state_list, axis=1))
```

```python
import functools

import jax
import jax.numpy as jnp
from jax import lax
from jax.experimental import pallas as pl
from jax.experimental.pallas import tpu as pltpu

F32 = jnp.float32
BF16 = jnp.bfloat16

D = 2048
DEPTH = 4
GRID_W = 64
H = 8
DK = 128
CONV_W = 5
CHUNK = 64
Q_LORA = 512
KV_LORA = 256
ROPE = 64
QK_DIM = DK + ROPE
ROPE_BASE = 10000.0
N_EXP = 16
D_FF = 512
N_MOD = 6
EPS = 1e-6
NEG = -1e30

LANE = 128
QKVZ = 4 * H * DK
TAIL = 1024
PROJ = QKVZ + TAIL
HQ = 2 * LANE

VMEM_LIMIT = 56 * 1024 * 1024


def _cp(*sem):
    return pltpu.CompilerParams(dimension_semantics=sem, vmem_limit_bytes=VMEM_LIMIT)


def _sigmoid(x):
    return 1.0 / (1.0 + jnp.exp(-x))


def _softplus(x):
    return jnp.maximum(x, 0.0) + jnp.log(1.0 + jnp.exp(-jnp.abs(x)))


def _mm(a, b):
    return jnp.dot(a.astype(BF16), b.astype(BF16), preferred_element_type=F32)


def _mm_f32(a, b):
    return jnp.dot(a, b, preferred_element_type=F32, precision=lax.Precision.HIGHEST)


def _mod_body(c_ref, w_ref, b_ref, o_ref):
    c = c_ref[...]
    o_ref[...] = _mm(c * _sigmoid(c), w_ref[...]) + b_ref[...]


def _modulation(conds, w_mod, b_mod):
    tn = 1024
    out = pl.pallas_call(
        _mod_body,
        grid=(DEPTH, N_MOD * D // tn),
        in_specs=[
            pl.BlockSpec((8, D), lambda l, j: (0, 0)),
            pl.BlockSpec((None, D, tn), lambda l, j: (l, 0, j)),
            pl.BlockSpec((None, 1, tn), lambda l, j: (l, 0, j)),
        ],
        out_specs=pl.BlockSpec((None, 8, tn), lambda l, j: (l, 0, j)),
        out_shape=jax.ShapeDtypeStruct((DEPTH, 8, N_MOD * D), F32),
        compiler_params=_cp("arbitrary", "arbitrary"),
        name="modulation",
    )(conds, w_mod, b_mod.reshape(DEPTH, 1, N_MOD * D))
    return out.reshape(DEPTH, 8, N_MOD, D)


def _in_body(x_ref, m_ref, g_ref, wm_ref, wt_ref, o_ref, h_scr, *, n_main):
    j = pl.program_id(1)

    @pl.when(j == 0)
    def _():
        x = x_ref[...]
        r = lax.rsqrt(jnp.mean(x * x, axis=-1, keepdims=True) + EPS)
        m = m_ref[...]
        h_scr[...] = (x * r * g_ref[...] * (1.0 + m[1:2, :]) + m[0:1, :]).astype(BF16)

    @pl.when(j < n_main)
    def _():
        o_ref[...] = jnp.dot(h_scr[...], wm_ref[...].astype(BF16), preferred_element_type=F32)

    @pl.when(j >= n_main)
    def _():
        o_ref[...] = jnp.dot(h_scr[...], wt_ref[...], preferred_element_type=F32)


def _in_proj(x, mods, g_mix, w_in, w_tail, l, cond_of_row):
    n = x.shape[0]
    tm, tn = 1024, 512
    n_main = QKVZ // tn
    n_tail = TAIL // tn
    return pl.pallas_call(
        functools.partial(_in_body, n_main=n_main),
        grid=(n // tm, n_main + n_tail),
        in_specs=[
            pl.BlockSpec((tm, D), lambda i, j: (i, 0)),
            pl.BlockSpec((None, None, N_MOD, D), lambda i, j: (l, cond_of_row(i * tm), 0, 0)),
            pl.BlockSpec((None, 1, D), lambda i, j: (l, 0, 0)),
            pl.BlockSpec((None, D, tn), lambda i, j: (l, 0, jnp.minimum(j, n_main - 1))),
            pl.BlockSpec((None, D, tn), lambda i, j: (l, 0, jnp.maximum(j - n_main, 0))),
        ],
        out_specs=pl.BlockSpec((tm, tn), lambda i, j: (i, j)),
        out_shape=jax.ShapeDtypeStruct((n, PROJ), F32),
        scratch_shapes=[pltpu.VMEM((tm, D), BF16)],
        compiler_params=_cp("arbitrary", "arbitrary"),
        name="in_proj",
    )(x, mods, g_mix.reshape(DEPTH, 1, D), w_in, w_tail)


def _bmm(a, b):
    return jnp.einsum('bij,bjk->bik', a.astype(BF16), b.astype(BF16), preferred_element_type=F32)


def _bmm_nt(a, b):
    return jnp.einsum('bid,bjd->bij', a.astype(BF16), b.astype(BF16), preferred_element_type=F32)


def _bmm_tn(a, b):
    return jnp.einsum('bci,bcj->bij', a.astype(BF16), b.astype(BF16), preferred_element_type=F32)


def _unit_triangular_inverse(lmat, ii, jj):
    b16 = (ii >> 4) == (jj >> 4)
    b32 = (ii >> 5) == (jj >> 5)
    eye = jnp.where(ii == jj, 1.0, 0.0)
    ld = jnp.where(b16, lmat, 0.0)
    l1 = jnp.where(b32, lmat - ld, 0.0)
    l2 = jnp.where(b32, 0.0, lmat)
    p = eye - ld
    a = _bmm(ld, ld)
    p = p + _bmm(p, a)
    a = _bmm(a, a)
    p = p + _bmm(p, a)
    a = _bmm(a, a)
    p = p + _bmm(p, a)
    t32 = p - _bmm(_bmm(p, l1), p)
    return t32 - _bmm(_bmm(t32, l2), t32)


def _dn_body(q_ref, k_ref, v_ref, z_ref, ab_ref, cwq_ref, cwk_ref, cwv_ref, alog_ref, dtb_ref, gout_ref, s0_ref,
             o_ref, sfin_ref, q_s, k_s, v_s, gc_s, beta_s, u_s, w_s, qd_s, kd_s, at_s, o_s, gate_s, *, t, nh, unroll):
    n_chunks = t // CHUNK
    span = unroll * CHUNK
    nb = nh * unroll
    row = lax.broadcasted_iota(jnp.int32, (t, LANE), 0)
    lane = lax.broadcasted_iota(jnp.int32, (t, LANE), 1)
    head0 = pl.program_id(1) * nh

    def conv_act(x, cw):
        acc = x * cw[2:3, :]
        for j in (0, 1, 3, 4):
            d = j - CONV_W // 2
            shifted = pltpu.roll(x, (-d) % t, axis=0)
            ok = jnp.logical_and(row + d >= 0, row + d < t)
            acc = acc + jnp.where(ok, shifted, 0.0) * cw[j:j + 1, :]
        return acc * _sigmoid(acc)

    def l2n(x):
        return x * lax.rsqrt(jnp.sum(x * x, axis=-1, keepdims=True) + EPS)

    @pl.when(pl.program_id(1) == 0)
    def _():
        ab = ab_ref[...]
        g_all = -jnp.exp(alog_ref[...]) * _softplus(ab + dtb_ref[...])
        pre = g_all
        suf = g_all
        r_in = row & (CHUNK - 1)
        s = 1
        while s < CHUNK:
            pre = pre + jnp.where(r_in >= s, pltpu.roll(pre, s, axis=0), 0.0)
            suf = suf + jnp.where(r_in < CHUNK - s, pltpu.roll(suf, t - s, axis=0), 0.0)
            s *= 2
        gate_s[0] = pre
        gate_s[1] = suf
        gate_s[2] = _sigmoid(ab)

    pre = gate_s[0]
    suf = gate_s[1]
    beta_all = gate_s[2]

    def column(x, idx):
        c = jnp.sum(jnp.where(lane == idx, x, 0.0), axis=1, keepdims=True)
        return jnp.broadcast_to(c, (t, LANE))

    for h in range(nh):
        sl = slice(h * LANE, (h + 1) * LANE)
        q_s[h] = l2n(conv_act(q_ref[:, sl], cwq_ref[:, sl])) * (DK ** -0.5)
        k_s[h] = l2n(conv_act(k_ref[:, sl], cwk_ref[:, sl]))
        v_s[h] = conv_act(v_ref[:, sl], cwv_ref[:, sl])
        gc_s[0, h] = column(pre, head0 + h)
        gc_s[1, h] = column(suf, H + head0 + h)
        beta_s[0, h] = column(beta_all, 2 * H + head0 + h)
        beta_s[1, h] = column(beta_all, 3 * H + head0 + h)
    o_s[...] = jnp.zeros_like(o_s)

    ii = lax.broadcasted_iota(jnp.int32, (1, CHUNK, CHUNK), 1)
    jj = lax.broadcasted_iota(jnp.int32, (1, CHUNK, CHUNK), 2)
    causal = ((ii >= jj, ii > jj), (ii <= jj, ii < jj))

    def chunk_end_gate(gc, d):
        return gc[:, CHUNK - 1:CHUNK, :] if d == 0 else gc[:, 0:1, :]

    def prepare_group(g, carry):
        rows = pl.ds(pl.multiple_of(g * span, span), span)
        per_chunk = lambda x: x.reshape(nb, CHUNK, x.shape[-1])
        per_head = lambda x: x.reshape(nh, span, x.shape[-1])
        q = per_chunk(q_s[:, rows, :])
        k = per_chunk(k_s[:, rows, :])
        v = per_chunk(v_s[:, rows, :])
        kk = _bmm_nt(k, k)
        qk = _bmm_nt(q, k)
        lmats, rhs = [], []
        for d in (0, 1):
            incl, strict = causal[d]
            gc = per_chunk(gc_s[d, :, rows, :])
            beta = per_chunk(beta_s[d, :, rows, :])
            gc_c = gc[:, :, :CHUNK]
            gc_row = jnp.sum(jnp.where(ii == jj, gc_c, 0.0), axis=1, keepdims=True)
            decay = jnp.exp(jnp.where(incl, gc_c - gc_row, NEG))
            e_gc = jnp.exp(gc)
            lmats.append(jnp.where(strict, kk * beta[:, :, :CHUNK] * decay, 0.0))
            rhs.append(jnp.concatenate([v * beta, k * beta * e_gc], axis=-1))
            qd_s[d, :, rows, :] = per_head(q * e_gc).astype(BF16)
            kd_s[d, :, rows, :] = per_head(k * jnp.exp(chunk_end_gate(gc, d) - gc)).astype(BF16)
            at_s[d, :, rows, :] = per_head(jnp.where(incl, qk * decay, 0.0)).astype(BF16)
        tmat = _unit_triangular_inverse(jnp.concatenate(lmats, axis=0), ii, jj)
        uw = _bmm(tmat, jnp.concatenate(rhs, axis=0))
        for d in (0, 1):
            part = uw[d * nb:(d + 1) * nb]
            u_s[d, :, rows, :] = per_head(part[:, :, :LANE])
            w_s[d, :, rows, :] = per_head(part[:, :, LANE:]).astype(BF16)
        return carry

    lax.fori_loop(0, n_chunks // unroll, prepare_group, 0)

    def scan_step(i, states):
        rows = [pl.ds(pl.multiple_of(c * CHUNK, CHUNK), CHUNK) for c in (i, n_chunks - 1 - i)]
        r = [_bmm(jnp.concatenate([w_s[d, :, rows[d], :], qd_s[d, :, rows[d], :]], axis=1), states[d])
             for d in (0, 1)]
        v_new = [(u_s[d, :, rows[d], :] - r[d][:, :CHUNK, :]).astype(BF16) for d in (0, 1)]
        o_add = [r[d][:, CHUNK:, :] + _bmm(at_s[d, :, rows[d], :], v_new[d]) for d in (0, 1)]
        s_add = [_bmm_tn(kd_s[d, :, rows[d], :], v_new[d]) for d in (0, 1)]
        out = []
        for d in (0, 1):
            o_s[:, rows[d], :] += o_add[d]
            out.append(states[d] * jnp.exp(chunk_end_gate(gc_s[d, :, rows[d], :], d)) + s_add[d])
        return tuple(out)

    sf, sb = lax.fori_loop(0, n_chunks, scan_step, (s0_ref[0], s0_ref[1]))
    sfin_ref[0] = sf
    sfin_ref[1] = sb

    for h in range(nh):
        sl = slice(h * LANE, (h + 1) * LANE)
        o = o_s[h]
        y = o * lax.rsqrt(jnp.mean(o * o, axis=-1, keepdims=True) + EPS) * gout_ref[...]
        z = z_ref[:, sl]
        o_ref[:, sl] = (y * z * _sigmoid(z)).astype(BF16)


def _deltanet(proj, conv_w, alog, dtb, g_out, s0, l, t, row0, nh, unroll):
    nseq = s0.shape[0]
    rb0 = row0 // t
    w = nh * LANE
    ng = H // nh
    tok = lambda part: pl.BlockSpec((t, w), lambda s, g: (rb0 + s, part * ng + g))
    cw = lambda part: pl.BlockSpec((None, CONV_W, w), lambda s, g: (l, 0, part * ng + g))
    small = pl.BlockSpec((None, 1, LANE), lambda s, g: (l, 0, 0))
    st = pl.BlockSpec((None, 2, nh, DK, DK), lambda s, g: (s, 0, g, 0, 0))
    wide = lambda dt: pltpu.VMEM((2, nh, t, LANE), dt)
    return pl.pallas_call(
        functools.partial(_dn_body, t=t, nh=nh, unroll=unroll),
        grid=(nseq, ng),
        in_specs=[tok(0), tok(1), tok(2), tok(3),
                  pl.BlockSpec((t, LANE), lambda s, g: (rb0 + s, PROJ // LANE - 1)),
                  cw(0), cw(1), cw(2), small, small, small, st],
        out_specs=[pl.BlockSpec((t, w), lambda s, g: (s, g)), st],
        out_shape=[jax.ShapeDtypeStruct((nseq * t, H * DK), BF16),
                   jax.ShapeDtypeStruct((nseq, 2, H, DK, DK), F32)],
        scratch_shapes=[pltpu.VMEM((nh, t, LANE), F32), pltpu.VMEM((nh, t, LANE), F32), pltpu.VMEM((nh, t, LANE), F32),
                        wide(F32), wide(F32), wide(F32), wide(BF16), wide(BF16), wide(BF16),
                        pltpu.VMEM((2, nh, t, CHUNK), BF16), pltpu.VMEM((nh, t, LANE), F32),
                        pltpu.VMEM((3, t, LANE), F32)],
        compiler_params=_cp("arbitrary", "arbitrary"),
        name="deltanet",
    )(proj, proj, proj, proj, proj, conv_w, conv_w, conv_w, alog, dtb, g_out.reshape(DEPTH, 1, DK), s0)


def _rope_swap(x, lane):
    return jnp.where((lane & 31) < 16, pltpu.roll(x, LANE - 16, axis=1), pltpu.roll(x, 16, axis=1))


def _head_keys_values(kv, kr, gk, cos, sin, lane, k_ref, v_ref):
    kr_ss = jnp.sum(kr * kr, axis=-1, keepdims=True)
    for h in range(H):
        kn = kv[:, h * HQ:h * HQ + DK]
        r = lax.rsqrt((jnp.sum(kn * kn, axis=-1, keepdims=True) + kr_ss) * (1.0 / QK_DIM) + EPS)
        k_ref[:, h * HQ:h * HQ + DK] = (kn * r * gk[:, :DK]).astype(BF16)
        rr = kr * r * gk[:, DK:]
        if cos is not None:
            rr = rr * cos + _rope_swap(rr, lane) * sin
        k_ref[:, h * HQ + DK:(h + 1) * HQ] = rr.astype(BF16)
        v_ref[:, h * DK:(h + 1) * DK] = kv[:, h * HQ + DK:(h + 1) * HQ].astype(BF16)


def _mla_prep_body(cq_ref, ckv_ref, kr_ref, wq_ref, wkv_ref, gqa_ref, gkva_ref, gq_ref, gk_ref, cos_ref, sin_ref,
                   q_ref, k_ref, v_ref, ckvn_ref, krope_ref):
    tm = cq_ref.shape[0]
    lane = lax.broadcasted_iota(jnp.int32, (tm, LANE), 1)
    cos = cos_ref[...]
    sin = sin_ref[...]
    cq = cq_ref[...]
    cqn = cq * lax.rsqrt(jnp.mean(cq * cq, axis=-1, keepdims=True) + EPS) * gqa_ref[...]
    q = jnp.dot(cqn.astype(BF16), wq_ref[...], preferred_element_type=F32)
    gq = gq_ref[...]
    for h in range(H):
        qn = q[:, h * HQ:h * HQ + DK]
        qr = q[:, h * HQ + DK:(h + 1) * HQ]
        ss = jnp.sum(qn * qn, axis=-1, keepdims=True) + jnp.sum(qr * qr, axis=-1, keepdims=True)
        r = lax.rsqrt(ss * (1.0 / QK_DIM) + EPS) * (QK_DIM ** -0.5)
        q_ref[:, h * HQ:h * HQ + DK] = (qn * r * gq[:, :DK]).astype(BF16)
        qr = qr * r * gq[:, DK:]
        q_ref[:, h * HQ + DK:(h + 1) * HQ] = (qr * cos + _rope_swap(qr, lane) * sin).astype(BF16)
    ckv = ckv_ref[...]
    ckvn = ckv * lax.rsqrt(jnp.mean(ckv * ckv, axis=-1, keepdims=True) + EPS) * gkva_ref[...]
    ckvn_ref[...] = ckvn
    kr = kr_ref[...]
    krope_ref[...] = kr[:, :ROPE]
    kv = jnp.dot(ckvn.astype(BF16), wkv_ref[...], preferred_element_type=F32)
    _head_keys_values(kv, kr, gk_ref[...], cos, sin, lane, k_ref, v_ref)


def _mla_prep(proj, wq, wkv, g_qa, g_kva, gq, gk, cos_t, sin_t, l, pos_block):
    n = proj.shape[0]
    tm = 512
    lw = lambda shape: pl.BlockSpec((None,) + shape, lambda i: (l,) + (0,) * len(shape))
    table = pl.BlockSpec((tm, LANE), lambda i: (pos_block(i * tm, tm), 0))
    return pl.pallas_call(
        _mla_prep_body,
        grid=(n // tm,),
        in_specs=[pl.BlockSpec((tm, Q_LORA), lambda i: (i, QKVZ // Q_LORA)),
                  pl.BlockSpec((tm, KV_LORA), lambda i: (i, (QKVZ + Q_LORA) // KV_LORA)),
                  pl.BlockSpec((tm, LANE), lambda i: (i, (QKVZ + Q_LORA + KV_LORA) // LANE)),
                  lw((Q_LORA, H * HQ)), lw((KV_LORA, H * HQ)), lw((1, Q_LORA)), lw((1, KV_LORA)),
                  lw((1, HQ)), lw((1, HQ)), table, table],
        out_specs=[pl.BlockSpec((tm, H * HQ), lambda i: (i, 0)), pl.BlockSpec((tm, H * HQ), lambda i: (i, 0)),
                   pl.BlockSpec((tm, H * DK), lambda i: (i, 0)), pl.BlockSpec((tm, KV_LORA), lambda i: (i, 0)),
                   pl.BlockSpec((tm, ROPE), lambda i: (i, 0))],
        out_shape=[jax.ShapeDtypeStruct((n, H * HQ), BF16), jax.ShapeDtypeStruct((n, H * HQ), BF16),
                   jax.ShapeDtypeStruct((n, H * DK), BF16), jax.ShapeDtypeStruct((n, KV_LORA), F32),
                   jax.ShapeDtypeStruct((n, ROPE), F32)],
        compiler_params=_cp("arbitrary"),
        name="mla_prep",
    )(proj, proj, proj, wq, wkv, g_qa, g_kva, gq, gk, cos_t, sin_t)


def _cache_prep_body(ckv_ref, kr_ref, wkv_ref, gk_ref, k_ref, v_ref):
    kv = jnp.dot(ckv_ref[...].astype(BF16), wkv_ref[...], preferred_element_type=F32)
    _head_keys_values(kv, kr_ref[...], gk_ref[...], None, None, None, k_ref, v_ref)


def _cache_prep(cache_ckv, cache_kr, wkv, gk, l):
    nb, _, past, _ = cache_ckv.shape
    return pl.pallas_call(
        _cache_prep_body,
        grid=(nb,),
        in_specs=[pl.BlockSpec((None, None, past, KV_LORA), lambda b: (b, l, 0, 0)),
                  pl.BlockSpec((None, None, past, LANE), lambda b: (b, l, 0, 0)),
                  pl.BlockSpec((None, KV_LORA, H * HQ), lambda b: (l, 0, 0)),
                  pl.BlockSpec((None, 1, HQ), lambda b: (l, 0, 0))],
        out_specs=[pl.BlockSpec((past, H * HQ), lambda b: (b, 0)), pl.BlockSpec((past, H * DK), lambda b: (b, 0))],
        out_shape=[jax.ShapeDtypeStruct((nb * past, H * HQ), BF16), jax.ShapeDtypeStruct((nb * past, H * DK), BF16)],
        compiler_params=_cp("arbitrary"),
        name="cache_prep",
    )(cache_ckv, cache_kr, wkv, gk)


def _attn_body(*refs, nparts, hb):
    q_ref = refs[0]
    k_refs = refs[1:1 + nparts]
    v_refs = refs[1 + nparts:1 + 2 * nparts]
    o_ref = refs[-1]
    for h in range(hb):
        q = q_ref[:, h * HQ:(h + 1) * HQ]
        scores = [lax.dot_general(q, k[:, h * HQ:(h + 1) * HQ], (((1,), (1,)), ((), ())), preferred_element_type=F32)
                  for k in k_refs]
        m = jnp.max(scores[0], axis=-1, keepdims=True)
        for sc in scores[1:]:
            m = jnp.maximum(m, jnp.max(sc, axis=-1, keepdims=True))
        num = 0.0
        den = 0.0
        for sc, v in zip(scores, v_refs):
            p = jnp.exp(sc - m)
            den = den + jnp.sum(p, axis=-1, keepdims=True)
            num = num + jnp.dot(p.astype(BF16), v[:, h * DK:(h + 1) * DK], preferred_element_type=F32)
        o_ref[:, h * DK:(h + 1) * DK] = (num / den).astype(BF16)


def _attention(q, parts, nseq, t, row0, tq, hb):
    nparts = len(parts)
    qb0 = row0 // tq
    nq = t // tq
    k_specs, v_specs, ks, vs = [], [], [], []
    for k_arr, v_arr, s_len, k_row0 in parts:
        kb0 = k_row0 // s_len
        k_specs.append(pl.BlockSpec((s_len, hb * HQ), lambda s, h, i, kb0=kb0: (kb0 + s, h)))
        v_specs.append(pl.BlockSpec((s_len, hb * DK), lambda s, h, i, kb0=kb0: (kb0 + s, h)))
        ks.append(k_arr)
        vs.append(v_arr)
    return pl.pallas_call(
        functools.partial(_attn_body, nparts=nparts, hb=hb),
        grid=(nseq, H // hb, nq),
        in_specs=[pl.BlockSpec((tq, hb * HQ), lambda s, h, i: (qb0 + s * nq + i, h))] + k_specs + v_specs,
        out_specs=pl.BlockSpec((tq, hb * DK), lambda s, h, i: (s * nq + i, h)),
        out_shape=jax.ShapeDtypeStruct((nseq * t, H * DK), BF16),
        compiler_params=_cp("arbitrary", "arbitrary", "arbitrary"),
        name="attention",
    )(q, *ks, *vs)


def _out_body(x_ref, dnc_ref, mlac_ref, dnl_ref, mlal_ref, w_ref, m_ref, o_ref, *, ctx_tiles):
    w = w_ref[...]

    def emit(dn_ref, mla_ref):
        y = jnp.dot(dn_ref[...], w[:H * DK, :], preferred_element_type=F32)
        y = y + jnp.dot(mla_ref[...], w[H * DK:, :], preferred_element_type=F32)
        o_ref[...] = x_ref[...] + m_ref[2:3, :] * y

    @pl.when(pl.program_id(0) < ctx_tiles)
    def _():
        emit(dnc_ref, mlac_ref)

    @pl.when(pl.program_id(0) >= ctx_tiles)
    def _():
        emit(dnl_ref, mlal_ref)


def _out_proj(x, dn_ctx, mla_ctx, dn_lat, mla_lat, w_out, mods, l, cond_of_row):
    n = x.shape[0]
    tm, tn = 512, 512
    ctx_tiles = dn_ctx.shape[0] // tm
    ctx = pl.BlockSpec((tm, H * DK), lambda i, j: (jnp.minimum(i, ctx_tiles - 1), 0))
    lat = pl.BlockSpec((tm, H * DK), lambda i, j: (jnp.maximum(i - ctx_tiles, 0), 0))
    return pl.pallas_call(
        functools.partial(_out_body, ctx_tiles=ctx_tiles),
        grid=(n // tm, D // tn),
        in_specs=[pl.BlockSpec((tm, tn), lambda i, j: (i, j)), ctx, ctx, lat, lat,
                  pl.BlockSpec((None, 2 * H * DK, tn), lambda i, j: (l, 0, j)),
                  pl.BlockSpec((None, None, N_MOD, tn), lambda i, j: (l, cond_of_row(i * tm), 0, j))],
        out_specs=pl.BlockSpec((tm, tn), lambda i, j: (i, j)),
        out_shape=jax.ShapeDtypeStruct((n, D), F32),
        compiler_params=_cp("arbitrary", "arbitrary"),
        name="out_proj",
    )(x, dn_ctx, mla_ctx, dn_lat, mla_lat, w_out, mods)


def _group_lane(x, k, lane):
    return jnp.where((lane & 3) + k < 4, pltpu.roll(x, LANE - k, axis=1), pltpu.roll(x, 4 - k, axis=1))


def _route_body(x_ref, m_ref, g_ref, wr_ref, br_ref, h_ref, rt_ref, cnt_ref, run_s):
    @pl.when(pl.program_id(0) == 0)
    def _():
        run_s[...] = jnp.zeros_like(run_s)

    x = x_ref[...]
    m = m_ref[...]
    h = x * lax.rsqrt(jnp.mean(x * x, axis=-1, keepdims=True) + EPS) * g_ref[...] * (1.0 + m[4:5, :]) + m[3:4, :]
    h_ref[...] = h
    tm = x.shape[0]
    lane = lax.broadcasted_iota(jnp.int32, (tm, LANE), 1)
    valid = lane < N_EXP
    scores = _sigmoid(_mm_f32(h, wr_ref[...]))
    sel = jnp.where(valid, scores + br_ref[...], NEG)
    rank = jnp.zeros((tm, LANE), F32)
    for k in (1, 2, 3):
        other = _group_lane(sel, k, lane)
        other_first = (lane & 3) + k >= 4
        beats = jnp.logical_or(other > sel, jnp.logical_and(other == sel, other_first))
        rank = rank + jnp.where(beats, 1.0, 0.0)
    top2 = rank < 2.0
    t = jnp.where(top2, sel, 0.0)
    gscore = t + _group_lane(t, 1, lane) + _group_lane(t, 2, lane) + _group_lane(t, 3, lane)
    lost = jnp.zeros((tm, LANE), F32)
    for k in (1, 2, 3):
        wrapped = lane + 4 * k >= N_EXP
        other = jnp.where(wrapped, pltpu.roll(gscore, N_EXP - 4 * k, axis=1), pltpu.roll(gscore, LANE - 4 * k, axis=1))
        loses = jnp.logical_or(other > gscore, jnp.logical_and(other == gscore, wrapped))
        lost = lost + jnp.where(loses, 1.0, 0.0)
    chosen = jnp.logical_and(jnp.logical_and(lost == 0.0, top2), valid)
    num = jnp.where(chosen, scores, 0.0)
    gate = num / jnp.sum(num, axis=-1, keepdims=True)
    member = jnp.where(chosen, 1.0, 0.0)
    r_i = lax.broadcasted_iota(jnp.int32, (tm, tm), 0)
    c_i = lax.broadcasted_iota(jnp.int32, (tm, tm), 1)
    earlier = jnp.where(r_i > c_i, 1.0, 0.0)
    pos = _mm(earlier, member) + run_s[0:1, :]
    run_s[...] = run_s[...] + jnp.sum(member, axis=0, keepdims=True)
    cnt_ref[...] = run_s[...]
    lane_f = lane.astype(F32)
    rec = jnp.zeros((tm, LANE), F32)
    for k in (0, 1):
        pick = jnp.logical_and(chosen, rank == float(k))
        for field, val in ((0, lane_f), (2, pos), (4, gate)):
            col = jnp.sum(jnp.where(pick, val, 0.0), axis=-1, keepdims=True)
            rec = jnp.where(lane == field + k, col, rec)
    rt_ref[...] = rec


def _route(x, mods, g_ffn, w_router, b_router, l, cond_of_row):
    n = x.shape[0]
    tm = 512
    return pl.pallas_call(
        _route_body,
        grid=(n // tm,),
        in_specs=[pl.BlockSpec((tm, D), lambda i: (i, 0)),
                  pl.BlockSpec((None, None, N_MOD, D), lambda i: (l, cond_of_row(i * tm), 0, 0)),
                  pl.BlockSpec((None, 1, D), lambda i: (l, 0, 0)),
                  pl.BlockSpec((D, LANE), lambda i: (0, 0)),
                  pl.BlockSpec((1, LANE), lambda i: (0, 0))],
        out_specs=[pl.BlockSpec((tm, D), lambda i: (i, 0)), pl.BlockSpec((tm, LANE), lambda i: (i, 0)),
                   pl.BlockSpec((8, LANE), lambda i: (0, 0))],
        out_shape=[jax.ShapeDtypeStruct((n, D), F32), jax.ShapeDtypeStruct((n, LANE), F32),
                   jax.ShapeDtypeStruct((8, LANE), F32)],
        scratch_shapes=[pltpu.VMEM((8, LANE), F32)],
        compiler_params=_cp("arbitrary"),
        name="route",
    )(x, mods, g_ffn.reshape(DEPTH, 1, D), w_router, b_router)


EXPERT_TILE = 512
TOP_K = 2


def _slot_plan(rt, cnt, n):
    n_tiles = (TOP_K * n) // EXPERT_TILE + N_EXP
    counts = cnt[0, :N_EXP].astype(jnp.int32)
    padded = ((counts + EXPERT_TILE - 1) // EXPERT_TILE) * EXPERT_TILE
    ends = jnp.cumsum(padded)
    starts = ends - padded
    expert = rt[:, 0:TOP_K].astype(jnp.int32)
    slots = (starts[expert] + rt[:, 2:2 + TOP_K].astype(jnp.int32)).T.reshape(-1)
    n_active = (ends[-1] // EXPERT_TILE).reshape(1)
    tile_expert = jnp.minimum(jnp.searchsorted(ends, jnp.arange(n_tiles) * EXPERT_TILE, side="right"), N_EXP - 1)
    token = jnp.tile(jnp.arange(n, dtype=jnp.int32), TOP_K)
    source = jnp.zeros((n_tiles * EXPERT_TILE,), jnp.int32).at[slots].set(token)
    return slots.astype(jnp.int32), source, tile_expert.astype(jnp.int32), n_active.astype(jnp.int32), n_tiles


def _experts_body(te_ref, na_ref, src_ref, h_hbm, wg_ref, wu_ref, wd_ref, y_ref, x_s, sem):
    j = pl.program_id(0)
    n_active = na_ref[0]
    quarter = EXPERT_TILE // 4

    def gather_start(tile, first, count):
        buf = tile & 1
        base = tile * EXPERT_TILE

        def issue(r, c):
            src = h_hbm.at[pl.ds(src_ref[base + r], 1)]
            pltpu.make_async_copy(src, x_s.at[buf, pl.ds(r, 1)], sem.at[buf]).start()
            return c
        lax.fori_loop(first, first + count, issue, 0, unroll=8)

    def prefetch_quarter(q):
        @pl.when(j + 1 < n_active)
        def _():
            gather_start(j + 1, q * quarter, quarter)

    def gather_wait(tile):
        buf = tile & 1
        pltpu.make_async_copy(h_hbm.at[pl.ds(0, EXPERT_TILE)], x_s.at[buf], sem.at[buf]).wait()

    @pl.when(j == 0)
    def _():
        gather_start(j, 0, EXPERT_TILE)

    @pl.when(j < n_active)
    def _():
        gather_wait(j)
        x = x_s[j & 1].astype(BF16)
        prefetch_quarter(0)
        hg = jnp.dot(x, wg_ref[...].astype(BF16), preferred_element_type=F32)
        prefetch_quarter(1)
        hu = jnp.dot(x, wu_ref[...].astype(BF16), preferred_element_type=F32)
        prefetch_quarter(2)
        act = (hg * _sigmoid(hg) * hu).astype(BF16)
        prefetch_quarter(3)
        y_ref[...] = jnp.dot(act, wd_ref[...].astype(BF16), preferred_element_type=F32)

    @pl.when(j >= n_active)
    def _():
        y_ref[...] = jnp.zeros_like(y_ref)


def _experts(h, source, tile_expert, n_active, w_gate, w_up, w_down, l, n_tiles):
    wspec = lambda a, b: pl.BlockSpec((None, None, a, b), lambda j, te, na, src: (l, te[j], 0, 0))
    return pl.pallas_call(
        _experts_body,
        grid_spec=pltpu.PrefetchScalarGridSpec(
            num_scalar_prefetch=3, grid=(n_tiles,),
            in_specs=[pl.BlockSpec(memory_space=pl.ANY), wspec(D, D_FF), wspec(D, D_FF), wspec(D_FF, D)],
            out_specs=pl.BlockSpec((EXPERT_TILE, D), lambda j, te, na, src: (j, 0)),
            scratch_shapes=[pltpu.VMEM((2, EXPERT_TILE, D), F32), pltpu.SemaphoreType.DMA((2,))]),
        out_shape=jax.ShapeDtypeStruct((n_tiles * EXPERT_TILE, D), F32),
        compiler_params=_cp("arbitrary"),
        name="experts",
    )(tile_expert, n_active, source, h, w_gate, w_up, w_down)


def _combine_body(slots_ref, y_hbm, x_ref, rt_ref, m_ref, o_ref, y_s, sem, *, n, tm):
    base = pl.program_id(0) * tm

    def issue(r, c):
        for k in range(TOP_K):
            src = y_hbm.at[pl.ds(slots_ref[k * n + base + r], 1)]
            pltpu.make_async_copy(src, y_s.at[k, pl.ds(r, 1)], sem).start()
        return c

    lax.fori_loop(0, tm, issue, 0, unroll=8)
    for k in range(TOP_K):
        pltpu.make_async_copy(y_hbm.at[pl.ds(0, tm)], y_s.at[k], sem).wait()
    rt = rt_ref[...]
    o_ref[...] = x_ref[...] + m_ref[5:6, :] * (rt[:, 4:5] * y_s[0] + rt[:, 5:6] * y_s[1])


def _combine(y, slots, x, rt, mods, l, cond_of_row):
    n = x.shape[0]
    tm = 256
    return pl.pallas_call(
        functools.partial(_combine_body, n=n, tm=tm),
        grid_spec=pltpu.PrefetchScalarGridSpec(
            num_scalar_prefetch=1, grid=(n // tm,),
            in_specs=[pl.BlockSpec(memory_space=pl.ANY),
                      pl.BlockSpec((tm, D), lambda i, s: (i, 0)),
                      pl.BlockSpec((tm, LANE), lambda i, s: (i, 0)),
                      pl.BlockSpec((None, None, N_MOD, D), lambda i, s: (l, cond_of_row(i * tm), 0, 0))],
            out_specs=pl.BlockSpec((tm, D), lambda i, s: (i, 0)),
            scratch_shapes=[pltpu.VMEM((TOP_K, tm, D), F32), pltpu.SemaphoreType.DMA(())]),
        out_shape=jax.ShapeDtypeStruct((n, D), F32),
        compiler_params=_cp("arbitrary"),
        name="combine",
    )(slots, y, x, rt, mods)


def _rope_tables(n_pos, n_identity):
    t = jnp.arange(n_pos)
    quarter = ROPE // 4
    inv = ROPE_BASE ** (-jnp.arange(quarter, dtype=F32) / quarter)
    ang_r = (t // GRID_W).astype(F32)[:, None] * inv
    ang_c = (t % GRID_W).astype(F32)[:, None] * inv
    one = jnp.ones((n_pos, LANE - ROPE), F32)
    cos = jnp.concatenate([jnp.cos(ang_r), jnp.cos(ang_r), jnp.cos(ang_c), jnp.cos(ang_c), one], axis=1)
    sin = jnp.concatenate([-jnp.sin(ang_r), jnp.sin(ang_r), -jnp.sin(ang_c), jnp.sin(ang_c), 0.0 * one], axis=1)
    cos = jnp.concatenate([cos, jnp.ones((n_identity, LANE), F32)], axis=0)
    sin = jnp.concatenate([sin, jnp.zeros((n_identity, LANE), F32)], axis=0)
    return cos, sin


def _pad_heads(w):
    lead = w.shape[:-1]
    w = w.reshape(lead + (H, QK_DIM))
    w = jnp.pad(w, [(0, 0)] * len(lead) + [(0, 0), (0, HQ - QK_DIM)])
    return w.reshape(lead + (H * HQ,))


def kernel(x_prompt, x_sample, cache_ckv, cache_krope, state_delta, c, c_ctx, g_mix, w_mod, b_mod, w_in, conv_w,
           a_log, dt_bias, g_dn_out, g_qa, w_uq, g_kva, w_ukv, g_qh, g_kh, w_out, g_ffn, w_router, b_router,
           w_gate, w_up, w_down):
    nb, seq, _ = x_prompt.shape
    ndb, dseq, _ = x_sample.shape
    n_ctx = nb * seq
    past = cache_ckv.shape[2]

    def cond_of_row(r):
        return jnp.where(r < n_ctx, 0, 1 + (r - n_ctx) // dseq)

    def pos_block(r, tm):
        return jnp.where(r < n_ctx, dseq // tm, ((r - n_ctx) % dseq) // tm)

    conds = jnp.concatenate([c_ctx[None, :], c, jnp.zeros((8 - 1 - ndb, D), F32)], axis=0)
    o_z = QKVZ
    o_a, o_b = o_z, o_z + 2 * H
    o_cq = o_b + 2 * H
    o_ckv = o_cq + Q_LORA
    o_kr = o_ckv + KV_LORA
    zc = lambda k: jnp.zeros((DEPTH, D, k), F32)
    w_tail = jnp.concatenate([w_in[:, :, o_cq:o_ckv], w_in[:, :, o_ckv:o_kr], w_in[:, :, o_kr:o_kr + ROPE],
                              zc(LANE - ROPE), w_in[:, :, o_a:o_cq], zc(LANE - 4 * H)], axis=2).astype(BF16)
    pad_lane = lambda v: jnp.pad(v.reshape(DEPTH, 1, -1), ((0, 0), (0, 0), (0, LANE - 2 * H)))
    alog_p = pad_lane(a_log)
    dtb_p = pad_lane(dt_bias)
    wq_p = _pad_heads(w_uq).astype(BF16)
    wkv_p = w_ukv.astype(BF16)
    gq_p = jnp.pad(g_qh, ((0, 0), (0, HQ - QK_DIM))).reshape(DEPTH, 1, HQ)
    gk_p = jnp.pad(g_kh, ((0, 0), (0, HQ - QK_DIM))).reshape(DEPTH, 1, HQ)
    g_qa_p = g_qa.reshape(DEPTH, 1, Q_LORA)
    g_kva_p = g_kva.reshape(DEPTH, 1, KV_LORA)
    cache_kr_p = jnp.pad(cache_krope, ((0, 0), (0, 0), (0, 0), (0, LANE - ROPE)))
    w_out_b = w_out.astype(BF16)
    wr_p = jnp.pad(w_router, ((0, 0), (0, LANE - N_EXP)))
    br_p = jnp.pad(b_router, (0, LANE - N_EXP)).reshape(1, LANE)
    cos_t, sin_t = _rope_tables(dseq, 512)
    s0_ctx = jnp.zeros((nb, 2, H, DK, DK), F32)

    mods = _modulation(conds, w_mod, b_mod)
    x = jnp.concatenate([x_prompt.reshape(n_ctx, D), x_sample.reshape(ndb * dseq, D)], axis=0)

    ckv_list, krope_list, state_list = [], [], []
    for l in range(DEPTH):
        proj = _in_proj(x, mods, g_mix, w_in, w_tail, l, cond_of_row)
        dn_ctx, s_ctx = _deltanet(proj, conv_w, alog_p, dtb_p, g_dn_out, s0_ctx, l, seq, 0, 8, 1)
        dn_lat, _ = _deltanet(proj, conv_w, alog_p, dtb_p, g_dn_out, state_delta[:, l], l, dseq, n_ctx, 2, 4)
        q, k, v, ckvn, krope = _mla_prep(proj, wq_p, wkv_p, g_qa_p, g_kva_p, gq_p, gk_p, cos_t, sin_t, l, pos_block)
        kc, vc = _cache_prep(cache_ckv, cache_kr_p, wkv_p, gk_p, l)
        mla_ctx = _attention(q, [(k, v, seq, 0)], nb, seq, 0, seq, H)
        mla_lat = _attention(q, [(kc, vc, past, 0), (k, v, dseq, n_ctx)], ndb, dseq, n_ctx, 512, 2)
        x = _out_proj(x, dn_ctx, mla_ctx, dn_lat, mla_lat, w_out_b, mods, l, cond_of_row)
        h, rt, cnt = _route(x, mods, g_ffn, wr_p, br_p, l, cond_of_row)
        slots, source, tile_expert, n_active, n_tiles = _slot_plan(rt, cnt, x.shape[0])
        y = _experts(h, source, tile_expert, n_active, w_gate, w_up, w_down, l, n_tiles)
        x = _combine(y, slots, x, rt, mods, l, cond_of_row)
        ckv_list.append(ckvn[:n_ctx].reshape(nb, seq, KV_LORA))
        krope_list.append(krope[:n_ctx].reshape(nb, seq, ROPE))
        state_list.append(s_ctx)

    y_prompt = x[:n_ctx].reshape(nb, seq, D)
    y_sample = x[n_ctx:].reshape(ndb, dseq, D)
    return (y_prompt, y_sample, jnp.stack(ckv_list, axis=1), jnp.stack(krope_list, axis=1),
            jnp.stack(state_list, axis=1))
```

```python
import functools

import jax
import jax.numpy as jnp
from jax import lax
from jax.experimental import pallas as pl
from jax.experimental.pallas import tpu as pltpu

F32 = jnp.float32
BF16 = jnp.bfloat16

D = 2048
DEPTH = 4
GRID_W = 64
H = 8
DK = 128
CONV_W = 5
CHUNK = 64
Q_LORA = 512
KV_LORA = 256
ROPE = 64
QK_DIM = DK + ROPE
ROPE_BASE = 10000.0
N_EXP = 16
D_FF = 512
N_MOD = 6
EPS = 1e-6
NEG = -1e30

LANE = 128
QKVZ = 4 * H * DK
TAIL = 1024
PROJ = QKVZ + TAIL
HQ = 2 * LANE

VMEM_LIMIT = 56 * 1024 * 1024


def _cp(*sem):
    return pltpu.CompilerParams(dimension_semantics=sem, vmem_limit_bytes=VMEM_LIMIT)


def _sigmoid(x):
    return 1.0 / (1.0 + jnp.exp(-x))


def _softplus(x):
    return jnp.maximum(x, 0.0) + jnp.log(1.0 + jnp.exp(-jnp.abs(x)))


def _mm(a, b):
    return jnp.dot(a.astype(BF16), b.astype(BF16), preferred_element_type=F32)


def _mm_f32(a, b):
    return jnp.dot(a, b, preferred_element_type=F32, precision=lax.Precision.HIGHEST)


def _mod_body(c_ref, w_ref, b_ref, o_ref):
    c = c_ref[...]
    o_ref[...] = _mm(c * _sigmoid(c), w_ref[...]) + b_ref[...]


def _modulation(conds, w_mod, b_mod):
    tn = 1024
    out = pl.pallas_call(
        _mod_body,
        grid=(DEPTH, N_MOD * D // tn),
        in_specs=[
            pl.BlockSpec((8, D), lambda l, j: (0, 0)),
            pl.BlockSpec((None, D, tn), lambda l, j: (l, 0, j)),
            pl.BlockSpec((None, 1, tn), lambda l, j: (l, 0, j)),
        ],
        out_specs=pl.BlockSpec((None, 8, tn), lambda l, j: (l, 0, j)),
        out_shape=jax.ShapeDtypeStruct((DEPTH, 8, N_MOD * D), F32),
        compiler_params=_cp("arbitrary", "arbitrary"),
        name="modulation",
    )(conds, w_mod, b_mod.reshape(DEPTH, 1, N_MOD * D))
    return out.reshape(DEPTH, 8, N_MOD, D)


def _in_body(x_ref, m_ref, g_ref, wm_ref, wt_ref, o_ref, h_scr, *, n_main):
    j = pl.program_id(1)

    @pl.when(j == 0)
    def _():
        x = x_ref[...]
        r = lax.rsqrt(jnp.mean(x * x, axis=-1, keepdims=True) + EPS)
        m = m_ref[...]
        h_scr[...] = (x * r * g_ref[...] * (1.0 + m[1:2, :]) + m[0:1, :]).astype(BF16)

    @pl.when(j < n_main)
    def _():
        o_ref[...] = jnp.dot(h_scr[...], wm_ref[...].astype(BF16), preferred_element_type=F32)

    @pl.when(j >= n_main)
    def _():
        o_ref[...] = jnp.dot(h_scr[...], wt_ref[...], preferred_element_type=F32)


def _in_proj(x, mods, g_mix, w_in, w_tail, l, cond_of_row):
    n = x.shape[0]
    tm, tn = 1024, 512
    n_main = QKVZ // tn
    n_tail = TAIL // tn
    return pl.pallas_call(
        functools.partial(_in_body, n_main=n_main),
        grid=(n // tm, n_main + n_tail),
        in_specs=[
            pl.BlockSpec((tm, D), lambda i, j: (i, 0)),
            pl.BlockSpec((None, None, N_MOD, D), lambda i, j: (l, cond_of_row(i * tm), 0, 0)),
            pl.BlockSpec((None, 1, D), lambda i, j: (l, 0, 0)),
            pl.BlockSpec((None, D, tn), lambda i, j: (l, 0, jnp.minimum(j, n_main - 1))),
            pl.BlockSpec((None, D, tn), lambda i, j: (l, 0, jnp.maximum(j - n_main, 0))),
        ],
        out_specs=pl.BlockSpec((tm, tn), lambda i, j: (i, j)),
        out_shape=jax.ShapeDtypeStruct((n, PROJ), F32),
        scratch_shapes=[pltpu.VMEM((tm, D), BF16)],
        compiler_params=_cp("arbitrary", "arbitrary"),
        name="in_proj",
    )(x, mods, g_mix.reshape(DEPTH, 1, D), w_in, w_tail)


def _bmm(a, b):
    return jnp.einsum('bij,bjk->bik', a.astype(BF16), b.astype(BF16), preferred_element_type=F32)


def _bmm_nt(a, b):
    return jnp.einsum('bid,bjd->bij', a.astype(BF16), b.astype(BF16), preferred_element_type=F32)


def _bmm_tn(a, b):
    return jnp.einsum('bci,bcj->bij', a.astype(BF16), b.astype(BF16), preferred_element_type=F32)


def _unit_triangular_inverse(lmat, ii, jj):
    b16 = (ii >> 4) == (jj >> 4)
    b32 = (ii >> 5) == (jj >> 5)
    eye = jnp.where(ii == jj, 1.0, 0.0)
    ld = jnp.where(b16, lmat, 0.0)
    l1 = jnp.where(b32, lmat - ld, 0.0)
    l2 = jnp.where(b32, 0.0, lmat)
    p = eye - ld
    a = _bmm(ld, ld)
    p = p + _bmm(p, a)
    a = _bmm(a, a)
    p = p + _bmm(p, a)
    a = _bmm(a, a)
    p = p + _bmm(p, a)
    t32 = p - _bmm(_bmm(p, l1), p)
    return t32 - _bmm(_bmm(t32, l2), t32)


def _dn_body(q_ref, k_ref, v_ref, z_ref, ab_ref, cwq_ref, cwk_ref, cwv_ref, alog_ref, dtb_ref, gout_ref, s0_ref,
             o_ref, sfin_ref, q_s, k_s, v_s, gc_s, beta_s, u_s, w_s, qd_s, kd_s, at_s, o_s, gate_s, *, t, nh, unroll):
    n_chunks = t // CHUNK
    span = unroll * CHUNK
    nb = nh * unroll
    row = lax.broadcasted_iota(jnp.int32, (t, LANE), 0)
    lane = lax.broadcasted_iota(jnp.int32, (t, LANE), 1)
    head0 = pl.program_id(1) * nh

    def conv_act(x, cw):
        acc = x * cw[2:3, :]
        for j in (0, 1, 3, 4):
            d = j - CONV_W // 2
            shifted = pltpu.roll(x, (-d) % t, axis=0)
            ok = jnp.logical_and(row + d >= 0, row + d < t)
            acc = acc + jnp.where(ok, shifted, 0.0) * cw[j:j + 1, :]
        return acc * _sigmoid(acc)

    def l2n(x):
        return x * lax.rsqrt(jnp.sum(x * x, axis=-1, keepdims=True) + EPS)

    @pl.when(pl.program_id(1) == 0)
    def _():
        ab = ab_ref[...]
        g_all = -jnp.exp(alog_ref[...]) * _softplus(ab + dtb_ref[...])
        pre = g_all
        suf = g_all
        r_in = row & (CHUNK - 1)
        s = 1
        while s < CHUNK:
            pre = pre + jnp.where(r_in >= s, pltpu.roll(pre, s, axis=0), 0.0)
            suf = suf + jnp.where(r_in < CHUNK - s, pltpu.roll(suf, t - s, axis=0), 0.0)
            s *= 2
        gate_s[0] = pre
        gate_s[1] = suf
        gate_s[2] = _sigmoid(ab)

    pre = gate_s[0]
    suf = gate_s[1]
    beta_all = gate_s[2]

    def column(x, idx):
        c = jnp.sum(jnp.where(lane == idx, x, 0.0), axis=1, keepdims=True)
        return jnp.broadcast_to(c, (t, LANE))

    for h in range(nh):
        sl = slice(h * LANE, (h + 1) * LANE)
        q_s[h] = l2n(conv_act(q_ref[:, sl], cwq_ref[:, sl])) * (DK ** -0.5)
        k_s[h] = l2n(conv_act(k_ref[:, sl], cwk_ref[:, sl]))
        v_s[h] = conv_act(v_ref[:, sl], cwv_ref[:, sl])
        gc_s[0, h] = column(pre, head0 + h)
        gc_s[1, h] = column(suf, H + head0 + h)
        beta_s[0, h] = column(beta_all, 2 * H + head0 + h)
        beta_s[1, h] = column(beta_all, 3 * H + head0 + h)
    o_s[...] = jnp.zeros_like(o_s)

    ii = lax.broadcasted_iota(jnp.int32, (1, CHUNK, CHUNK), 1)
    jj = lax.broadcasted_iota(jnp.int32, (1, CHUNK, CHUNK), 2)
    causal = ((ii >= jj, ii > jj), (ii <= jj, ii < jj))

    def chunk_end_gate(gc, d):
        return gc[:, CHUNK - 1:CHUNK, :] if d == 0 else gc[:, 0:1, :]

    def prepare_group(g, carry):
        rows = pl.ds(pl.multiple_of(g * span, span), span)
        per_chunk = lambda x: x.reshape(nb, CHUNK, x.shape[-1])
        per_head = lambda x: x.reshape(nh, span, x.shape[-1])
        q = per_chunk(q_s[:, rows, :])
        k = per_chunk(k_s[:, rows, :])
        v = per_chunk(v_s[:, rows, :])
        kk = _bmm_nt(k, k)
        qk = _bmm_nt(q, k)
        lmats, rhs = [], []
        for d in (0, 1):
            incl, strict = causal[d]
            gc = per_chunk(gc_s[d, :, rows, :])
            beta = per_chunk(beta_s[d, :, rows, :])
            gc_c = gc[:, :, :CHUNK]
            gc_row = jnp.sum(jnp.where(ii == jj, gc_c, 0.0), axis=1, keepdims=True)
            decay = jnp.exp(jnp.where(incl, gc_c - gc_row, NEG))
            e_gc = jnp.exp(gc)
            lmats.append(jnp.where(strict, kk * beta[:, :, :CHUNK] * decay, 0.0))
            rhs.append(jnp.concatenate([v * beta, k * beta * e_gc], axis=-1))
            qd_s[d, :, rows, :] = per_head(q * e_gc).astype(BF16)
            kd_s[d, :, rows, :] = per_head(k * jnp.exp(chunk_end_gate(gc, d) - gc)).astype(BF16)
            at_s[d, :, rows, :] = per_head(jnp.where(incl, qk * decay, 0.0)).astype(BF16)
        tmat = _unit_triangular_inverse(jnp.concatenate(lmats, axis=0), ii, jj)
        uw = _bmm(tmat, jnp.concatenate(rhs, axis=0))
        for d in (0, 1):
            part = uw[d * nb:(d + 1) * nb]
            u_s[d, :, rows, :] = per_head(part[:, :, :LANE])
            w_s[d, :, rows, :] = per_head(part[:, :, LANE:]).astype(BF16)
        return carry

    lax.fori_loop(0, n_chunks // unroll, prepare_group, 0)

    def scan_step(i, states):
        rows = [pl.ds(pl.multiple_of(c * CHUNK, CHUNK), CHUNK) for c in (i, n_chunks - 1 - i)]
        r = [_bmm(jnp.concatenate([w_s[d, :, rows[d], :], qd_s[d, :, rows[d], :]], axis=1), states[d])
             for d in (0, 1)]
        v_new = [(u_s[d, :, rows[d], :] - r[d][:, :CHUNK, :]).astype(BF16) for d in (0, 1)]
        o_add = [r[d][:, CHUNK:, :] + _bmm(at_s[d, :, rows[d], :], v_new[d]) for d in (0, 1)]
        s_add = [_bmm_tn(kd_s[d, :, rows[d], :], v_new[d]) for d in (0, 1)]
        out = []
        for d in (0, 1):
            o_s[:, rows[d], :] += o_add[d]
            out.append(states[d] * jnp.exp(chunk_end_gate(gc_s[d, :, rows[d], :], d)) + s_add[d])
        return tuple(out)

    sf, sb = lax.fori_loop(0, n_chunks, scan_step, (s0_ref[0], s0_ref[1]))
    sfin_ref[0] = sf
    sfin_ref[1] = sb

    for h in range(nh):
        sl = slice(h * LANE, (h + 1) * LANE)
        o = o_s[h]
        y = o * lax.rsqrt(jnp.mean(o * o, axis=-1, keepdims=True) + EPS) * gout_ref[...]
        z = z_ref[:, sl]
        o_ref[:, sl] = (y * z * _sigmoid(z)).astype(BF16)


def _deltanet(proj, conv_w, alog, dtb, g_out, s0, l, t, row0, nh, unroll):
    nseq = s0.shape[0]
    rb0 = row0 // t
    w = nh * LANE
    ng = H // nh
    tok = lambda part: pl.BlockSpec((t, w), lambda s, g: (rb0 + s, part * ng + g))
    cw = lambda part: pl.BlockSpec((None, CONV_W, w), lambda s, g: (l, 0, part * ng + g))
    small = pl.BlockSpec((None, 1, LANE), lambda s, g: (l, 0, 0))
    st = pl.BlockSpec((None, 2, nh, DK, DK), lambda s, g: (s, 0, g, 0, 0))
    wide = lambda dt: pltpu.VMEM((2, nh, t, LANE), dt)
    return pl.pallas_call(
        functools.partial(_dn_body, t=t, nh=nh, unroll=unroll),
        grid=(nseq, ng),
        in_specs=[tok(0), tok(1), tok(2), tok(3),
                  pl.BlockSpec((t, LANE), lambda s, g: (rb0 + s, PROJ // LANE - 1)),
                  cw(0), cw(1), cw(2), small, small, small, st],
        out_specs=[pl.BlockSpec((t, w), lambda s, g: (s, g)), st],
        out_shape=[jax.ShapeDtypeStruct((nseq * t, H * DK), BF16),
                   jax.ShapeDtypeStruct((nseq, 2, H, DK, DK), F32)],
        scratch_shapes=[pltpu.VMEM((nh, t, LANE), F32), pltpu.VMEM((nh, t, LANE), F32), pltpu.VMEM((nh, t, LANE), F32),
                        wide(F32), wide(F32), wide(F32), wide(BF16), wide(BF16), wide(BF16),
                        pltpu.VMEM((2, nh, t, CHUNK), BF16), pltpu.VMEM((nh, t, LANE), F32),
                        pltpu.VMEM((3, t, LANE), F32)],
        compiler_params=_cp("arbitrary", "arbitrary"),
        name="deltanet",
    )(proj, proj, proj, proj, proj, conv_w, conv_w, conv_w, alog, dtb, g_out.reshape(DEPTH, 1, DK), s0)


def _rope_swap(x, lane):
    return jnp.where((lane & 31) < 16, pltpu.roll(x, LANE - 16, axis=1), pltpu.roll(x, 16, axis=1))


def _head_keys_values(kv, kr, gk, cos, sin, lane, k_ref, v_ref):
    kr_ss = jnp.sum(kr * kr, axis=-1, keepdims=True)
    for h in range(H):
        kn = kv[:, h * HQ:h * HQ + DK]
        r = lax.rsqrt((jnp.sum(kn * kn, axis=-1, keepdims=True) + kr_ss) * (1.0 / QK_DIM) + EPS)
        k_ref[:, h * HQ:h * HQ + DK] = (kn * r * gk[:, :DK]).astype(BF16)
        rr = kr * r * gk[:, DK:]
        if cos is not None:
            rr = rr * cos + _rope_swap(rr, lane) * sin
        k_ref[:, h * HQ + DK:(h + 1) * HQ] = rr.astype(BF16)
        v_ref[:, h * DK:(h + 1) * DK] = kv[:, h * HQ + DK:(h + 1) * HQ].astype(BF16)


def _mla_prep_body(cq_ref, ckv_ref, kr_ref, wq_ref, wkv_ref, gqa_ref, gkva_ref, gq_ref, gk_ref, cos_ref, sin_ref,
                   q_ref, k_ref, v_ref, ckvn_ref, krope_ref):
    tm = cq_ref.shape[0]
    lane = lax.broadcasted_iota(jnp.int32, (tm, LANE), 1)
    cos = cos_ref[...]
    sin = sin_ref[...]
    cq = cq_ref[...]
    cqn = cq * lax.rsqrt(jnp.mean(cq * cq, axis=-1, keepdims=True) + EPS) * gqa_ref[...]
    q = jnp.dot(cqn.astype(BF16), wq_ref[...], preferred_element_type=F32)
    gq = gq_ref[...]
    for h in range(H):
        qn = q[:, h * HQ:h * HQ + DK]
        qr = q[:, h * HQ + DK:(h + 1) * HQ]
        ss = jnp.sum(qn * qn, axis=-1, keepdims=True) + jnp.sum(qr * qr, axis=-1, keepdims=True)
        r = lax.rsqrt(ss * (1.0 / QK_DIM) + EPS) * (QK_DIM ** -0.5)
        q_ref[:, h * HQ:h * HQ + DK] = (qn * r * gq[:, :DK]).astype(BF16)
        qr = qr * r * gq[:, DK:]
        q_ref[:, h * HQ + DK:(h + 1) * HQ] = (qr * cos + _rope_swap(qr, lane) * sin).astype(BF16)
    ckv = ckv_ref[...]
    ckvn = ckv * lax.rsqrt(jnp.mean(ckv * ckv, axis=-1, keepdims=True) + EPS) * gkva_ref[...]
    ckvn_ref[...] = ckvn
    kr = kr_ref[...]
    krope_ref[...] = kr[:, :ROPE]
    kv = jnp.dot(ckvn.astype(BF16), wkv_ref[...], preferred_element_type=F32)
    _head_keys_values(kv, kr, gk_ref[...], cos, sin, lane, k_ref, v_ref)


def _mla_prep(proj, wq, wkv, g_qa, g_kva, gq, gk, cos_t, sin_t, l, pos_block):
    n = proj.shape[0]
    tm = 512
    lw = lambda shape: pl.BlockSpec((None,) + shape, lambda i: (l,) + (0,) * len(shape))
    table = pl.BlockSpec((tm, LANE), lambda i: (pos_block(i * tm, tm), 0))
    return pl.pallas_call(
        _mla_prep_body,
        grid=(n // tm,),
        in_specs=[pl.BlockSpec((tm, Q_LORA), lambda i: (i, QKVZ // Q_LORA)),
                  pl.BlockSpec((tm, KV_LORA), lambda i: (i, (QKVZ + Q_LORA) // KV_LORA)),
                  pl.BlockSpec((tm, LANE), lambda i: (i, (QKVZ + Q_LORA + KV_LORA) // LANE)),
                  lw((Q_LORA, H * HQ)), lw((KV_LORA, H * HQ)), lw((1, Q_LORA)), lw((1, KV_LORA)),
                  lw((1, HQ)), lw((1, HQ)), table, table],
        out_specs=[pl.BlockSpec((tm, H * HQ), lambda i: (i, 0)), pl.BlockSpec((tm, H * HQ), lambda i: (i, 0)),
                   pl.BlockSpec((tm, H * DK), lambda i: (i, 0)), pl.BlockSpec((tm, KV_LORA), lambda i: (i, 0)),
                   pl.BlockSpec((tm, ROPE), lambda i: (i, 0))],
        out_shape=[jax.ShapeDtypeStruct((n, H * HQ), BF16), jax.ShapeDtypeStruct((n, H * HQ), BF16),
                   jax.ShapeDtypeStruct((n, H * DK), BF16), jax.ShapeDtypeStruct((n, KV_LORA), F32),
                   jax.ShapeDtypeStruct((n, ROPE), F32)],
        compiler_params=_cp("arbitrary"),
        name="mla_prep",
    )(proj, proj, proj, wq, wkv, g_qa, g_kva, gq, gk, cos_t, sin_t)


def _cache_prep_body(ckv_ref, kr_ref, wkv_ref, gk_ref, k_ref, v_ref):
    kv = jnp.dot(ckv_ref[...].astype(BF16), wkv_ref[...], preferred_element_type=F32)
    _head_keys_values(kv, kr_ref[...], gk_ref[...], None, None, None, k_ref, v_ref)


def _cache_prep(cache_ckv, cache_kr, wkv, gk, l):
    nb, _, past, _ = cache_ckv.shape
    return pl.pallas_call(
        _cache_prep_body,
        grid=(nb,),
        in_specs=[pl.BlockSpec((None, None, past, KV_LORA), lambda b: (b, l, 0, 0)),
                  pl.BlockSpec((None, None, past, LANE), lambda b: (b, l, 0, 0)),
                  pl.BlockSpec((None, KV_LORA, H * HQ), lambda b: (l, 0, 0)),
                  pl.BlockSpec((None, 1, HQ), lambda b: (l, 0, 0))],
        out_specs=[pl.BlockSpec((past, H * HQ), lambda b: (b, 0)), pl.BlockSpec((past, H * DK), lambda b: (b, 0))],
        out_shape=[jax.ShapeDtypeStruct((nb * past, H * HQ), BF16), jax.ShapeDtypeStruct((nb * past, H * DK), BF16)],
        compiler_params=_cp("arbitrary"),
        name="cache_prep",
    )(cache_ckv, cache_kr, wkv, gk)


def _attn_body(*refs, nparts, hb):
    q_ref = refs[0]
    k_refs = refs[1:1 + nparts]
    v_refs = refs[1 + nparts:1 + 2 * nparts]
    o_ref = refs[-1]
    for h in range(hb):
        q = q_ref[:, h * HQ:(h + 1) * HQ]
        scores = [lax.dot_general(q, k[:, h * HQ:(h + 1) * HQ], (((1,), (1,)), ((), ())), preferred_element_type=F32)
                  for k in k_refs]
        m = jnp.max(scores[0], axis=-1, keepdims=True)
        for sc in scores[1:]:
            m = jnp.maximum(m, jnp.max(sc, axis=-1, keepdims=True))
        num = 0.0
        den = 0.0
        for sc, v in zip(scores, v_refs):
            p = jnp.exp(sc - m)
            den = den + jnp.sum(p, axis=-1, keepdims=True)
            num = num + jnp.dot(p.astype(BF16), v[:, h * DK:(h + 1) * DK], preferred_element_type=F32)
        o_ref[:, h * DK:(h + 1) * DK] = (num / den).astype(BF16)


def _attention(q, parts, nseq, t, row0, tq, hb):
    nparts = len(parts)
    qb0 = row0 // tq
    nq = t // tq
    k_specs, v_specs, ks, vs = [], [], [], []
    for k_arr, v_arr, s_len, k_row0 in parts:
        kb0 = k_row0 // s_len
        k_specs.append(pl.BlockSpec((s_len, hb * HQ), lambda s, h, i, kb0=kb0: (kb0 + s, h)))
        v_specs.append(pl.BlockSpec((s_len, hb * DK), lambda s, h, i, kb0=kb0: (kb0 + s, h)))
        ks.append(k_arr)
        vs.append(v_arr)
    return pl.pallas_call(
        functools.partial(_attn_body, nparts=nparts, hb=hb),
        grid=(nseq, H // hb, nq),
        in_specs=[pl.BlockSpec((tq, hb * HQ), lambda s, h, i: (qb0 + s * nq + i, h))] + k_specs + v_specs,
        out_specs=pl.BlockSpec((tq, hb * DK), lambda s, h, i: (s * nq + i, h)),
        out_shape=jax.ShapeDtypeStruct((nseq * t, H * DK), BF16),
        compiler_params=_cp("arbitrary", "arbitrary", "arbitrary"),
        name="attention",
    )(q, *ks, *vs)


def _out_body(x_ref, dnc_ref, mlac_ref, dnl_ref, mlal_ref, w_ref, m_ref, o_ref, *, ctx_tiles):
    w = w_ref[...]

    def emit(dn_ref, mla_ref):
        y = jnp.dot(dn_ref[...], w[:H * DK, :], preferred_element_type=F32)
        y = y + jnp.dot(mla_ref[...], w[H * DK:, :], preferred_element_type=F32)
        o_ref[...] = x_ref[...] + m_ref[2:3, :] * y

    @pl.when(pl.program_id(0) < ctx_tiles)
    def _():
        emit(dnc_ref, mlac_ref)

    @pl.when(pl.program_id(0) >= ctx_tiles)
    def _():
        emit(dnl_ref, mlal_ref)


def _out_proj(x, dn_ctx, mla_ctx, dn_lat, mla_lat, w_out, mods, l, cond_of_row):
    n = x.shape[0]
    tm, tn = 512, 512
    ctx_tiles = dn_ctx.shape[0] // tm
    ctx = pl.BlockSpec((tm, H * DK), lambda i, j: (jnp.minimum(i, ctx_tiles - 1), 0))
    lat = pl.BlockSpec((tm, H * DK), lambda i, j: (jnp.maximum(i - ctx_tiles, 0), 0))
    return pl.pallas_call(
        functools.partial(_out_body, ctx_tiles=ctx_tiles),
        grid=(n // tm, D // tn),
        in_specs=[pl.BlockSpec((tm, tn), lambda i, j: (i, j)), ctx, ctx, lat, lat,
                  pl.BlockSpec((None, 2 * H * DK, tn), lambda i, j: (l, 0, j)),
                  pl.BlockSpec((None, None, N_MOD, tn), lambda i, j: (l, cond_of_row(i * tm), 0, j))],
        out_specs=pl.BlockSpec((tm, tn), lambda i, j: (i, j)),
        out_shape=jax.ShapeDtypeStruct((n, D), F32),
        compiler_params=_cp("arbitrary", "arbitrary"),
        name="out_proj",
    )(x, dn_ctx, mla_ctx, dn_lat, mla_lat, w_out, mods)


def _group_lane(x, k, lane):
    return jnp.where((lane & 3) + k < 4, pltpu.roll(x, LANE - k, axis=1), pltpu.roll(x, 4 - k, axis=1))


def _route_body(x_ref, m_ref, g_ref, wr_ref, br_ref, h_ref, rt_ref, cnt_ref, run_s):
    @pl.when(pl.program_id(0) == 0)
    def _():
        run_s[...] = jnp.zeros_like(run_s)

    x = x_ref[...]
    m = m_ref[...]
    h = x * lax.rsqrt(jnp.mean(x * x, axis=-1, keepdims=True) + EPS) * g_ref[...] * (1.0 + m[4:5, :]) + m[3:4, :]
    h_ref[...] = h
    tm = x.shape[0]
    lane = lax.broadcasted_iota(jnp.int32, (tm, LANE), 1)
    valid = lane < N_EXP
    scores = _sigmoid(_mm_f32(h, wr_ref[...]))
    sel = jnp.where(valid, scores + br_ref[...], NEG)
    rank = jnp.zeros((tm, LANE), F32)
    for k in (1, 2, 3):
        other = _group_lane(sel, k, lane)
        other_first = (lane & 3) + k >= 4
        beats = jnp.logical_or(other > sel, jnp.logical_and(other == sel, other_first))
        rank = rank + jnp.where(beats, 1.0, 0.0)
    top2 = rank < 2.0
    t = jnp.where(top2, sel, 0.0)
    gscore = t + _group_lane(t, 1, lane) + _group_lane(t, 2, lane) + _group_lane(t, 3, lane)
    lost = jnp.zeros((tm, LANE), F32)
    for k in (1, 2, 3):
        wrapped = lane + 4 * k >= N_EXP
        other = jnp.where(wrapped, pltpu.roll(gscore, N_EXP - 4 * k, axis=1), pltpu.roll(gscore, LANE - 4 * k, axis=1))
        loses = jnp.logical_or(other > gscore, jnp.logical_and(other == gscore, wrapped))
        lost = lost + jnp.where(loses, 1.0, 0.0)
    chosen = jnp.logical_and(jnp.logical_and(lost == 0.0, top2), valid)
    num = jnp.where(chosen, scores, 0.0)
    gate = num / jnp.sum(num, axis=-1, keepdims=True)
    member = jnp.where(chosen, 1.0, 0.0)
    r_i = lax.broadcasted_iota(jnp.int32, (tm, tm), 0)
    c_i = lax.broadcasted_iota(jnp.int32, (tm, tm), 1)
    earlier = jnp.where(r_i > c_i, 1.0, 0.0)
    pos = _mm(earlier, member) + run_s[0:1, :]
    run_s[...] = run_s[...] + jnp.sum(member, axis=0, keepdims=True)
    cnt_ref[...] = run_s[...]
    lane_f = lane.astype(F32)
    rec = jnp.zeros((tm, LANE), F32)
    for k in (0, 1):
        pick = jnp.logical_and(chosen, rank == float(k))
        for field, val in ((0, lane_f), (2, pos), (4, gate)):
            col = jnp.sum(jnp.where(pick, val, 0.0), axis=-1, keepdims=True)
            rec = jnp.where(lane == field + k, col, rec)
    rt_ref[...] = rec


def _route(x, mods, g_ffn, w_router, b_router, l, cond_of_row):
    n = x.shape[0]
    tm = 512
    return pl.pallas_call(
        _route_body,
        grid=(n // tm,),
        in_specs=[pl.BlockSpec((tm, D), lambda i: (i, 0)),
                  pl.BlockSpec((None, None, N_MOD, D), lambda i: (l, cond_of_row(i * tm), 0, 0)),
                  pl.BlockSpec((None, 1, D), lambda i: (l, 0, 0)),
                  pl.BlockSpec((D, LANE), lambda i: (0, 0)),
                  pl.BlockSpec((1, LANE), lambda i: (0, 0))],
        out_specs=[pl.BlockSpec((tm, D), lambda i: (i, 0)), pl.BlockSpec((tm, LANE), lambda i: (i, 0)),
                   pl.BlockSpec((8, LANE), lambda i: (0, 0))],
        out_shape=[jax.ShapeDtypeStruct((n, D), F32), jax.ShapeDtypeStruct((n, LANE), F32),
                   jax.ShapeDtypeStruct((8, LANE), F32)],
        scratch_shapes=[pltpu.VMEM((8, LANE), F32)],
        compiler_params=_cp("arbitrary"),
        name="route",
    )(x, mods, g_ffn.reshape(DEPTH, 1, D), w_router, b_router)


EXPERT_TILE = 512
TOP_K = 2
GATHER_AHEAD = 2


def _slot_plan(rt, cnt, n):
    n_tiles = (TOP_K * n) // EXPERT_TILE + N_EXP
    counts = cnt[0, :N_EXP].astype(jnp.int32)
    padded = ((counts + EXPERT_TILE - 1) // EXPERT_TILE) * EXPERT_TILE
    ends = jnp.cumsum(padded)
    starts = ends - padded
    expert = rt[:, 0:TOP_K].astype(jnp.int32)
    slots = (starts[expert] + rt[:, 2:2 + TOP_K].astype(jnp.int32)).T.reshape(-1)
    n_active = (ends[-1] // EXPERT_TILE).reshape(1)
    tile_expert = jnp.minimum(jnp.searchsorted(ends, jnp.arange(n_tiles) * EXPERT_TILE, side="right"), N_EXP - 1)
    token = jnp.tile(jnp.arange(n, dtype=jnp.int32), TOP_K)
    source = jnp.zeros((n_tiles * EXPERT_TILE,), jnp.int32).at[slots].set(token)
    return slots.astype(jnp.int32), source, tile_expert.astype(jnp.int32), n_active.astype(jnp.int32), n_tiles


def _experts_body(te_ref, na_ref, src_ref, h_hbm, wg_ref, wu_ref, wd_ref, y_ref, x_s, sem):
    j = pl.program_id(0)
    n_active = na_ref[0]
    n_buf = GATHER_AHEAD + 1

    def gather_start(tile):
        buf = lax.rem(tile, n_buf)
        base = tile * EXPERT_TILE

        def issue(i, c):
            for p in range(2):
                r = 2 * i + p
                src = h_hbm.at[pl.ds(src_ref[base + r], 1)]
                pltpu.make_async_copy(src, x_s.at[buf, pl.ds(r, 1)], sem.at[buf]).start(priority=p)
            return c
        lax.fori_loop(0, EXPERT_TILE // 2, issue, 0, unroll=4)

    def gather_wait(tile):
        buf = lax.rem(tile, n_buf)
        pltpu.make_async_copy(h_hbm.at[pl.ds(0, EXPERT_TILE)], x_s.at[buf], sem.at[buf]).wait()

    @pl.when(j == 0)
    def _():
        for t in range(GATHER_AHEAD):
            @pl.when(t < n_active)
            def _():
                gather_start(jnp.int32(t))

    @pl.when(j + GATHER_AHEAD < n_active)
    def _():
        gather_start(j + GATHER_AHEAD)

    @pl.when(j < n_active)
    def _():
        gather_wait(j)
        x = x_s[lax.rem(j, n_buf)].astype(BF16)
        hg = jnp.dot(x, wg_ref[...].astype(BF16), preferred_element_type=F32)
        hu = jnp.dot(x, wu_ref[...].astype(BF16), preferred_element_type=F32)
        act = (hg * _sigmoid(hg) * hu).astype(BF16)
        y_ref[...] = jnp.dot(act, wd_ref[...].astype(BF16), preferred_element_type=F32)

    @pl.when(j >= n_active)
    def _():
        y_ref[...] = jnp.zeros_like(y_ref)


def _experts(h, source, tile_expert, n_active, w_gate, w_up, w_down, l, n_tiles):
    wspec = lambda a, b: pl.BlockSpec((None, None, a, b), lambda j, te, na, src: (l, te[j], 0, 0))
    return pl.pallas_call(
        _experts_body,
        grid_spec=pltpu.PrefetchScalarGridSpec(
            num_scalar_prefetch=3, grid=(n_tiles,),
            in_specs=[pl.BlockSpec(memory_space=pl.ANY), wspec(D, D_FF), wspec(D, D_FF), wspec(D_FF, D)],
            out_specs=pl.BlockSpec((EXPERT_TILE, D), lambda j, te, na, src: (j, 0)),
            scratch_shapes=[pltpu.VMEM((GATHER_AHEAD + 1, EXPERT_TILE, D), F32),
                            pltpu.SemaphoreType.DMA((GATHER_AHEAD + 1,))]),
        out_shape=jax.ShapeDtypeStruct((n_tiles * EXPERT_TILE, D), F32),
        compiler_params=_cp("arbitrary"),
        name="experts",
    )(tile_expert, n_active, source, h, w_gate, w_up, w_down)


def _combine_body(slots_ref, y_hbm, x_ref, rt_ref, m_ref, o_ref, y_s, sem, *, n, tm):
    base = pl.program_id(0) * tm

    def issue(r, c):
        for k in range(TOP_K):
            src = y_hbm.at[pl.ds(slots_ref[k * n + base + r], 1)]
            pltpu.make_async_copy(src, y_s.at[k, pl.ds(r, 1)], sem).start(priority=k)
        return c

    lax.fori_loop(0, tm, issue, 0, unroll=8)
    for k in range(TOP_K):
        pltpu.make_async_copy(y_hbm.at[pl.ds(0, tm)], y_s.at[k], sem).wait()
    rt = rt_ref[...]
    o_ref[...] = x_ref[...] + m_ref[5:6, :] * (rt[:, 4:5] * y_s[0] + rt[:, 5:6] * y_s[1])


def _combine(y, slots, x, rt, mods, l, cond_of_row):
    n = x.shape[0]
    tm = 256
    return pl.pallas_call(
        functools.partial(_combine_body, n=n, tm=tm),
        grid_spec=pltpu.PrefetchScalarGridSpec(
            num_scalar_prefetch=1, grid=(n // tm,),
            in_specs=[pl.BlockSpec(memory_space=pl.ANY),
                      pl.BlockSpec((tm, D), lambda i, s: (i, 0)),
                      pl.BlockSpec((tm, LANE), lambda i, s: (i, 0)),
                      pl.BlockSpec((None, None, N_MOD, D), lambda i, s: (l, cond_of_row(i * tm), 0, 0))],
            out_specs=pl.BlockSpec((tm, D), lambda i, s: (i, 0)),
            scratch_shapes=[pltpu.VMEM((TOP_K, tm, D), F32), pltpu.SemaphoreType.DMA(())]),
        out_shape=jax.ShapeDtypeStruct((n, D), F32),
        compiler_params=_cp("arbitrary"),
        name="combine",
    )(slots, y, x, rt, mods)


def _rope_tables(n_pos, n_identity):
    t = jnp.arange(n_pos)
    quarter = ROPE // 4
    inv = ROPE_BASE ** (-jnp.arange(quarter, dtype=F32) / quarter)
    ang_r = (t // GRID_W).astype(F32)[:, None] * inv
    ang_c = (t % GRID_W).astype(F32)[:, None] * inv
    one = jnp.ones((n_pos, LANE - ROPE), F32)
    cos = jnp.concatenate([jnp.cos(ang_r), jnp.cos(ang_r), jnp.cos(ang_c), jnp.cos(ang_c), one], axis=1)
    sin = jnp.concatenate([-jnp.sin(ang_r), jnp.sin(ang_r), -jnp.sin(ang_c), jnp.sin(ang_c), 0.0 * one], axis=1)
    cos = jnp.concatenate([cos, jnp.ones((n_identity, LANE), F32)], axis=0)
    sin = jnp.concatenate([sin, jnp.zeros((n_identity, LANE), F32)], axis=0)
    return cos, sin


def _pad_heads(w):
    lead = w.shape[:-1]
    w = w.reshape(lead + (H, QK_DIM))
    w = jnp.pad(w, [(0, 0)] * len(lead) + [(0, 0), (0, HQ - QK_DIM)])
    return w.reshape(lead + (H * HQ,))


def kernel(x_prompt, x_sample, cache_ckv, cache_krope, state_delta, c, c_ctx, g_mix, w_mod, b_mod, w_in, conv_w,
           a_log, dt_bias, g_dn_out, g_qa, w_uq, g_kva, w_ukv, g_qh, g_kh, w_out, g_ffn, w_router, b_router,
           w_gate, w_up, w_down):
    nb, seq, _ = x_prompt.shape
    ndb, dseq, _ = x_sample.shape
    n_ctx = nb * seq
    past = cache_ckv.shape[2]

    def cond_of_row(r):
        return jnp.where(r < n_ctx, 0, 1 + (r - n_ctx) // dseq)

    def pos_block(r, tm):
        return jnp.where(r < n_ctx, dseq // tm, ((r - n_ctx) % dseq) // tm)

    conds = jnp.concatenate([c_ctx[None, :], c, jnp.zeros((8 - 1 - ndb, D), F32)], axis=0)
    o_z = QKVZ
    o_a, o_b = o_z, o_z + 2 * H
    o_cq = o_b + 2 * H
    o_ckv = o_cq + Q_LORA
    o_kr = o_ckv + KV_LORA
    zc = lambda k: jnp.zeros((DEPTH, D, k), F32)
    w_tail = jnp.concatenate([w_in[:, :, o_cq:o_ckv], w_in[:, :, o_ckv:o_kr], w_in[:, :, o_kr:o_kr + ROPE],
                              zc(LANE - ROPE), w_in[:, :, o_a:o_cq], zc(LANE - 4 * H)], axis=2).astype(BF16)
    pad_lane = lambda v: jnp.pad(v.reshape(DEPTH, 1, -1), ((0, 0), (0, 0), (0, LANE - 2 * H)))
    alog_p = pad_lane(a_log)
    dtb_p = pad_lane(dt_bias)
    wq_p = _pad_heads(w_uq).astype(BF16)
    wkv_p = w_ukv.astype(BF16)
    gq_p = jnp.pad(g_qh, ((0, 0), (0, HQ - QK_DIM))).reshape(DEPTH, 1, HQ)
    gk_p = jnp.pad(g_kh, ((0, 0), (0, HQ - QK_DIM))).reshape(DEPTH, 1, HQ)
    g_qa_p = g_qa.reshape(DEPTH, 1, Q_LORA)
    g_kva_p = g_kva.reshape(DEPTH, 1, KV_LORA)
    cache_kr_p = jnp.pad(cache_krope, ((0, 0), (0, 0), (0, 0), (0, LANE - ROPE)))
    w_out_b = w_out.astype(BF16)
    wr_p = jnp.pad(w_router, ((0, 0), (0, LANE - N_EXP)))
    br_p = jnp.pad(b_router, (0, LANE - N_EXP)).reshape(1, LANE)
    cos_t, sin_t = _rope_tables(dseq, 512)
    s0_ctx = jnp.zeros((nb, 2, H, DK, DK), F32)

    mods = _modulation(conds, w_mod, b_mod)
    x = jnp.concatenate([x_prompt.reshape(n_ctx, D), x_sample.reshape(ndb * dseq, D)], axis=0)

    ckv_list, krope_list, state_list = [], [], []
    for l in range(DEPTH):
        proj = _in_proj(x, mods, g_mix, w_in, w_tail, l, cond_of_row)
        dn_ctx, s_ctx = _deltanet(proj, conv_w, alog_p, dtb_p, g_dn_out, s0_ctx, l, seq, 0, 8, 1)
        dn_lat, _ = _deltanet(proj, conv_w, alog_p, dtb_p, g_dn_out, state_delta[:, l], l, dseq, n_ctx, 2, 4)
        q, k, v, ckvn, krope = _mla_prep(proj, wq_p, wkv_p, g_qa_p, g_kva_p, gq_p, gk_p, cos_t, sin_t, l, pos_block)
        kc, vc = _cache_prep(cache_ckv, cache_kr_p, wkv_p, gk_p, l)
        mla_ctx = _attention(q, [(k, v, seq, 0)], nb, seq, 0, seq, H)
        mla_lat = _attention(q, [(kc, vc, past, 0), (k, v, dseq, n_ctx)], ndb, dseq, n_ctx, 512, 2)
        x = _out_proj(x, dn_ctx, mla_ctx, dn_lat, mla_lat, w_out_b, mods, l, cond_of_row)
        h, rt, cnt = _route(x, mods, g_ffn, wr_p, br_p, l, cond_of_row)
        slots, source, tile_expert, n_active, n_tiles = _slot_plan(rt, cnt, x.shape[0])
        y = _experts(h, source, tile_expert, n_active, w_gate, w_up, w_down, l, n_tiles)
        x = _combine(y, slots, x, rt, mods, l, cond_of_row)
        ckv_list.append(ckvn[:n_ctx].reshape(nb, seq, KV_LORA))
        krope_list.append(krope[:n_ctx].reshape(nb, seq, ROPE))
        state_list.append(s_ctx)

    y_prompt = x[:n_ctx].reshape(nb, seq, D)
    y_sample = x[n_ctx:].reshape(ndb, dseq, D)
    return (y_prompt, y_sample, jnp.stack(ckv_list, axis=1), jnp.stack(krope_list, axis=1),
            jnp.stack(state_list, axis=1))
```
